```python
import math
import jax
import jax.numpy as jnp
from jax import lax
import numpy as np

D_MODEL = 1024
BATCH = 4
SEQ = 8192
DEPTH = 2
DEC_BATCH = 32
DEC_SEQ = 4
PAST_LEN = 16384
PAGE_SIZE = 128

N_MIXERS = 2
N_GLA = (DEPTH + 1) // 2
N_FOX = DEPTH // 2
GLA_HEADS = 4
GLA_DK = D_MODEL // 2 // GLA_HEADS
GLA_DV = D_MODEL // GLA_HEADS
GLA_GATE_RANK = 16
GLA_TAU = 16.0
GLA_CHUNK = 64
FOX_HEADS = 16
FOX_HD = D_MODEL // FOX_HEADS
FOX_BLOCK = 128
FOX_GATE_BIAS = 3.0
PEER_HEADS = 8
PEER_NKEYS = 128
PEER_N_EXPERTS = PEER_NKEYS * PEER_NKEYS
PEER_QDIM = 256
PEER_TOPK = 16
PEER_BLOCK = 256
RMS_EPS = 1e-6

kernel_name = 'gla_fox_peer_hybrid_step'


def rmsnorm(x, g):
    xf = x.astype(jnp.float32)
    y = xf * lax.rsqrt(jnp.mean(xf * xf, axis=-1, keepdims=True) + RMS_EPS)
    return (y * g.astype(jnp.float32)).astype(x.dtype)


def gla_scan(q, k, v, logg, s0, chunk):
    B, S, H, DK = q.shape
    n = S // chunk

    def blocks(a):
        return a.reshape(B, n, chunk, H, a.shape[-1]).swapaxes(0, 1)

    causal = jnp.tril(jnp.ones((chunk, chunk), dtype=bool))[None, :, :, None, None]

    def step(s, inp):
        qc, kc, vc, gc = inp
        b = jnp.cumsum(gc, axis=1)
        decay = jnp.exp(jnp.where(causal, b[:, :, None] - b[:, None, :], -jnp.inf))
        att = jnp.einsum('btshd,bshd->bhts', qc[:, :, None] * decay, kc)
        o = jnp.einsum('bhts,bshv->bthv', att, vc) + jnp.einsum('bthd,bhdv->bthv', qc * jnp.exp(b), s)
        b_last = b[:, -1]
        s_new = jnp.exp(b_last)[..., None] * s + jnp.einsum('bshd,bshv->bhdv', kc * jnp.exp(b_last[:, None] - b), vc)
        return s_new, o

    s_fin, o = lax.scan(step, s0, (blocks(q), blocks(k), blocks(v), blocks(logg)))
    o = o.swapaxes(0, 1).reshape(B, S, H, v.shape[-1])
    return o, s_fin


def gla_mixer(h, s0, w_q, w_k, w_v, w_g1, w_g2, b_g, w_r, b_r, g_o, w_o, chunk):
    B, S, _ = h.shape
    f32 = jnp.float32
    q = (h @ w_q).astype(f32).reshape(B, S, GLA_HEADS, GLA_DK) * GLA_DK ** -0.5
    k = (h @ w_k).astype(f32).reshape(B, S, GLA_HEADS, GLA_DK)
    v = (h @ w_v).astype(f32).reshape(B, S, GLA_HEADS, GLA_DV)
    logg = jax.nn.log_sigmoid(((h @ w_g1) @ w_g2 + b_g).astype(f32)).reshape(B, S, GLA_HEADS, GLA_DK) / GLA_TAU
    o, s_fin = gla_scan(q, k, v, logg, s0.astype(f32), chunk)
    o = rmsnorm(o, g_o).reshape(B, S, GLA_HEADS * GLA_DV).astype(h.dtype)
    r = jax.nn.silu(h @ w_r + b_r)
    return (o * r) @ w_o, s_fin.astype(s0.dtype)


def fox_proj(h, w_q, w_k, w_v, w_f, b_f):
    B, S, _ = h.shape
    q = (h @ w_q).reshape(B, S, FOX_HEADS, FOX_HD)
    k = (h @ w_k).reshape(B, S, FOX_HEADS, FOX_HD)
    v = (h @ w_v).reshape(B, S, FOX_HEADS, FOX_HD)
    logf = jax.nn.log_sigmoid((h @ w_f + b_f).astype(jnp.float32))
    return q, k, v, logf


def fox_prompt(q, k, v, logf):
    B, S, H, HD = q.shape
    nb = S // FOX_BLOCK
    ct = jnp.cumsum(logf, axis=1).transpose(0, 2, 1)
    qb = q.reshape(B, nb, FOX_BLOCK, H, HD).swapaxes(0, 1)
    cb = ct.reshape(B, H, nb, FOX_BLOCK).transpose(2, 0, 1, 3)
    starts = jnp.arange(nb) * FOX_BLOCK
    kpos = jnp.arange(S)

    def block(args):
        qi, ci, t0 = args
        logits = jnp.einsum('bqhd,bkhd->bhqk', qi, k).astype(jnp.float32) * FOX_HD ** -0.5
        logits = logits + ci[..., None] - ct[:, :, None, :]
        mask = (t0 + jnp.arange(FOX_BLOCK))[:, None] >= kpos[None, :]
        p = jax.nn.softmax(jnp.where(mask, logits, -jnp.inf), axis=-1)
        return jnp.einsum('bhqk,bkhd->bqhd', p.astype(v.dtype), v)

    o = lax.map(block, (qb, cb, starts))
    return o.swapaxes(0, 1).reshape(B, S, H * HD)


def fox_sample(q, k, v, logf, cache_k, cache_v, cache_logf, page_table, j):
    DB, T, H, HD = q.shape
    kp = cache_k[j, page_table].reshape(DB, -1, H, HD)
    vp = cache_v[j, page_table].reshape(DB, -1, H, HD)
    lfp = cache_logf[j, page_table].reshape(DB, -1, H).astype(jnp.float32)
    P = kp.shape[1]
    r = jnp.cumsum(lfp[:, ::-1], axis=1)[:, ::-1] - lfp
    cn = jnp.cumsum(logf, axis=1)
    cq = cn.transpose(0, 2, 1)[..., None]
    scale = FOX_HD ** -0.5
    lp = jnp.einsum('bthd,bshd->bhts', q, kp).astype(jnp.float32) * scale + cq + r.transpose(0, 2, 1)[:, :, None, :]
    ln = jnp.einsum('bthd,bshd->bhts', q, k).astype(jnp.float32) * scale + cq - cn.transpose(0, 2, 1)[:, :, None, :]
    ln = jnp.where(jnp.tril(jnp.ones((T, T), dtype=bool)), ln, -jnp.inf)
    p = jax.nn.softmax(jnp.concatenate([lp, ln], axis=-1), axis=-1).astype(v.dtype)
    o = jnp.einsum('bhts,bshd->bthd', p[..., :P], vp) + jnp.einsum('bhts,bshd->bthd', p[..., P:], v)
    return o.reshape(DB, T, H * HD)


def peer_ffn(h, w_pq, sub_keys, u, v):
    shape = h.shape
    D = shape[-1]
    xt = h.reshape(-1, D)
    n = xt.shape[0]
    pad = (-n) % PEER_BLOCK
    xt = jnp.pad(xt, ((0, pad), (0, 0))).reshape(-1, PEER_BLOCK, D)

    def block(xb):
        q = (xb @ w_pq).reshape(PEER_BLOCK, PEER_HEADS, 2, PEER_QDIM // 2)
        s = jnp.einsum('nphd,phkd->nphk', q, sub_keys).astype(jnp.float32)
        sc, ix = lax.top_k(s, PEER_TOPK)
        cand = (sc[:, :, 0, :, None] + sc[:, :, 1, None, :]).reshape(PEER_BLOCK, PEER_HEADS, PEER_TOPK * PEER_TOPK)
        best, pos = lax.top_k(cand, PEER_TOPK)
        e = (jnp.take_along_axis(ix[:, :, 0], pos // PEER_TOPK, axis=-1) * PEER_NKEYS
             + jnp.take_along_axis(ix[:, :, 1], pos % PEER_TOPK, axis=-1))
        g = jax.nn.softmax(best, axis=-1)
        act = jax.nn.gelu(jnp.einsum('nd,nped->npe', xb, u[e]), approximate=False)
        return jnp.einsum('npe,nped->nd', (g * act.astype(jnp.float32)).astype(xb.dtype), v[e])

    out = lax.map(block, xt).reshape(-1, D)[:n]
    return out.reshape(shape)


def setup_inputs(seed: int = 0) -> dict:
    key = jax.random.key(seed)
    keys = jax.random.split(key, 30)
    f32 = jnp.float32

    def nrm(i, shape, scale):
        return jax.random.normal(keys[i], shape, f32) * scale

    D = D_MODEL
    n_pages = PAST_LEN // PAGE_SIZE
    n_pool = (DEC_BATCH * n_pages * 5) // 4
    gk = GLA_HEADS * GLA_DK
    gv = GLA_HEADS * GLA_DV
    fw = FOX_HEADS * FOX_HD
    pq = PEER_HEADS * PEER_QDIM
    page_table = jax.random.permutation(keys[6], n_pool)[: DEC_BATCH * n_pages].reshape(DEC_BATCH, n_pages).astype(jnp.int32)
    return {
        'x_prompt': nrm(0, (BATCH, SEQ, D), 1.0),
        'x_sample': nrm(1, (DEC_BATCH, DEC_SEQ, D), 1.0),
        'state_gla': nrm(2, (N_GLA, DEC_BATCH, GLA_HEADS, GLA_DK, GLA_DV), 0.5),
        'cache_k': nrm(3, (N_FOX, n_pool, PAGE_SIZE, FOX_HEADS, FOX_HD), 1.0),
        'cache_v': nrm(4, (N_FOX, n_pool, PAGE_SIZE, FOX_HEADS, FOX_HD), 1.0),
        'cache_logf': jax.nn.log_sigmoid(FOX_GATE_BIAS + nrm(5, (N_FOX, n_pool, PAGE_SIZE, FOX_HEADS), 0.5)),
        'page_table': page_table,
        'g_mix': 1.0 + nrm(7, (DEPTH, D), 0.05),
        'g_ffn': 1.0 + nrm(8, (DEPTH, D), 0.05),
        'g_final': 1.0 + nrm(9, (D,), 0.05),
        'gla_w_q': nrm(10, (N_GLA, D, gk), D ** -0.5),
        'gla_w_k': nrm(11, (N_GLA, D, gk), D ** -0.5),
        'gla_w_v': nrm(12, (N_GLA, D, gv), D ** -0.5),
        'gla_w_g1': nrm(13, (N_GLA, D, GLA_GATE_RANK), D ** -0.5),
        'gla_w_g2': nrm(14, (N_GLA, GLA_GATE_RANK, gk), GLA_GATE_RANK ** -0.5),
        'gla_b_g': nrm(15, (N_GLA, gk), 0.1),
        'gla_w_r': nrm(16, (N_GLA, D, gv), D ** -0.5),
        'gla_b_r': nrm(17, (N_GLA, gv), 0.02),
        'gla_g_o': 1.0 + nrm(18, (N_GLA, GLA_DV), 0.05),
        'gla_w_o': nrm(19, (N_GLA, gv, D), gv ** -0.5),
        'fox_w_q': nrm(20, (N_FOX, D, fw), D ** -0.5),
        'fox_w_k': nrm(21, (N_FOX, D, fw), D ** -0.5),
        'fox_w_v': nrm(22, (N_FOX, D, fw), D ** -0.5),
        'fox_w_f': nrm(23, (N_FOX, D, FOX_HEADS), 0.5 * D ** -0.5),
        'fox_b_f': FOX_GATE_BIAS + nrm(24, (N_FOX, FOX_HEADS), 0.1),
        'fox_w_o': nrm(25, (N_FOX, fw, D), fw ** -0.5),
        'peer_w_q': nrm(26, (DEPTH, D, pq), D ** -0.5),
        'peer_keys': nrm(27, (DEPTH, PEER_HEADS, 2, PEER_NKEYS, PEER_QDIM // 2), (PEER_QDIM // 2) ** -0.5),
        'peer_u': nrm(28, (DEPTH, PEER_N_EXPERTS, D), D ** -0.5),
        'peer_v': nrm(29, (DEPTH, PEER_N_EXPERTS, D), PEER_HEADS ** -0.5),
    }


def reference(x_prompt, x_sample, state_gla, cache_k, cache_v, cache_logf, page_table,
              g_mix, g_ffn, g_final,
              gla_w_q, gla_w_k, gla_w_v, gla_w_g1, gla_w_g2, gla_b_g, gla_w_r, gla_b_r, gla_g_o, gla_w_o,
              fox_w_q, fox_w_k, fox_w_v, fox_w_f, fox_b_f, fox_w_o,
              peer_w_q, peer_keys, peer_u, peer_v):
    xp, xs = x_prompt, x_sample
    gla_sp, gla_ss = [], []
    fox_kp, fox_vp, fox_lp, fox_ks, fox_vs, fox_ls = [], [], [], [], [], []
    for i in range(DEPTH):
        hp = rmsnorm(xp, g_mix[i])
        hs = rmsnorm(xs, g_mix[i])
        j = i // N_MIXERS
        if i % N_MIXERS == 0:
            w = (gla_w_q[j], gla_w_k[j], gla_w_v[j], gla_w_g1[j], gla_w_g2[j], gla_b_g[j],
                 gla_w_r[j], gla_b_r[j], gla_g_o[j], gla_w_o[j])
            s0 = jnp.zeros((xp.shape[0], GLA_HEADS, GLA_DK, GLA_DV), state_gla.dtype)
            yp, sp = gla_mixer(hp, s0, *w, chunk=GLA_CHUNK)
            ys, ss = gla_mixer(hs, state_gla[j], *w, chunk=xs.shape[1])
            gla_sp.append(sp)
            gla_ss.append(ss)
        else:
            qp, kp, vp, lp = fox_proj(hp, fox_w_q[j], fox_w_k[j], fox_w_v[j], fox_w_f[j], fox_b_f[j])
            yp = fox_prompt(qp, kp, vp, lp) @ fox_w_o[j]
            qs, ks, vs, ls = fox_proj(hs, fox_w_q[j], fox_w_k[j], fox_w_v[j], fox_w_f[j], fox_b_f[j])
            ys = fox_sample(qs, ks, vs, ls, cache_k, cache_v, cache_logf, page_table, j) @ fox_w_o[j]
            fox_kp.append(kp)
            fox_vp.append(vp)
            fox_lp.append(lp)
            fox_ks.append(ks)
            fox_vs.append(vs)
            fox_ls.append(ls)
        xp = xp + yp
        xs = xs + ys
        xp = xp + peer_ffn(rmsnorm(xp, g_ffn[i]), peer_w_q[i], peer_keys[i], peer_u[i], peer_v[i])
        xs = xs + peer_ffn(rmsnorm(xs, g_ffn[i]), peer_w_q[i], peer_keys[i], peer_u[i], peer_v[i])
    y_prompt = rmsnorm(xp, g_final)
    y_sample = rmsnorm(xs, g_final)
    return (y_prompt, y_sample, jnp.stack(gla_sp), jnp.stack(gla_ss),
            jnp.stack(fox_kp), jnp.stack(fox_vp), jnp.stack(fox_lp),
            jnp.stack(fox_ks), jnp.stack(fox_vs), jnp.stack(fox_ls))
```

```python
import functools

import jax
import jax.numpy as jnp
from jax import lax
from jax.experimental import pallas as pl
from jax.experimental.pallas import tpu as pltpu

F32 = jnp.float32
BF16 = jnp.bfloat16
HIGHEST = lax.Precision.HIGHEST

D_MODEL = 1024
GLA_HEADS = 4
GLA_DK = 128
GLA_DV = 256
GLA_TAU = 16.0
GLA_CHUNK = 64
GLA_SUB = 16
FOX_HEADS = 16
FOX_HD = 64
FOX_PAIRS = FOX_HEADS * FOX_HD // 128
PEER_HEADS = 8
PEER_NKEYS = 128
PEER_TOPK = 16
PEER_N_EXPERTS = PEER_NKEYS * PEER_NKEYS
RMS_EPS = 1e-6
LANES = 128
VMEM_LIMIT = 56 * 1024 * 1024


def _params(n_axes, vmem=VMEM_LIMIT):
    return pltpu.CompilerParams(dimension_semantics=("arbitrary",) * n_axes, vmem_limit_bytes=vmem)


def _rms(x, g):
    return x * lax.rsqrt(jnp.mean(x * x, axis=-1, keepdims=True) + RMS_EPS) * g


def _dot(a, b, precision=None):
    return jnp.dot(a, b, preferred_element_type=F32, precision=precision)


def _dot_nt(a, b):
    return lax.dot_general(a, b, (((1,), (1,)), ((), ())), preferred_element_type=F32)


def _dot_tn(a, b):
    return lax.dot_general(a, b, (((0,), (0,)), ((), ())), preferred_element_type=F32)


def _log_sigmoid(z):
    return jnp.minimum(z, 0.0) - jnp.log1p(jnp.exp(-jnp.abs(z)))


def _const_spec(shape):
    return pl.BlockSpec(shape, lambda *_: (0,) * len(shape))


def _gla_proj_kernel(x_ref, g_ref, wq_ref, wk_ref, wv_ref, wr_ref, wg1_ref, wg2_ref, bg_ref, br_ref,
                     q_ref, k_ref, v_ref, lg_ref, r_ref):
    h = _rms(x_ref[...], g_ref[...]).astype(BF16)
    q_ref[...] = _dot(h, wq_ref[...]) * (GLA_DK ** -0.5)
    k_ref[...] = _dot(h, wk_ref[...])
    v_ref[...] = _dot(h, wv_ref[...])
    u = _dot(h, wr_ref[...]) + br_ref[...]
    r_ref[...] = u / (1.0 + jnp.exp(-u))
    t = _dot(h, wg1_ref[...]).astype(BF16)
    z = _dot(t, wg2_ref[...]) + bg_ref[...]
    lg_ref[...] = _log_sigmoid(z) * (1.0 / GLA_TAU)


def _gla_proj(x, g, wq, wk, wv, wr, wg1, wg2, bg, br, tm):
    n, d = x.shape
    gk = GLA_HEADS * GLA_DK
    gv = GLA_HEADS * GLA_DV
    tok = lambda w: pl.BlockSpec((tm, w), lambda i: (i, 0))
    return pl.pallas_call(
        _gla_proj_kernel,
        grid=(n // tm,),
        in_specs=[tok(d), _const_spec((1, d)), _const_spec((d, gk)), _const_spec((d, gk)), _const_spec((d, gv)),
                  _const_spec((d, gv)), _const_spec((d, LANES)), _const_spec((LANES, gk)), _const_spec((1, gk)),
                  _const_spec((1, gv))],
        out_specs=[tok(gk), tok(gk), tok(gv), tok(gk), tok(gv)],
        out_shape=[jax.ShapeDtypeStruct((n, gk), F32), jax.ShapeDtypeStruct((n, gk), F32),
                   jax.ShapeDtypeStruct((n, gv), F32), jax.ShapeDtypeStruct((n, gk), F32),
                   jax.ShapeDtypeStruct((n, gv), F32)],
        compiler_params=_params(1),
        name="gla_proj",
    )(x, g, wq, wk, wv, wr, wg1, wg2, bg, br)


def _gla_scan_kernel(q_ref, k_ref, v_ref, g_ref, s0_ref, o_ref, sfin_ref, st_ref, *, chunk, sub):
    j = pl.program_id(1)

    @pl.when(j == 0)
    def _():
        st_ref[...] = s0_ref[0]

    rows = lax.broadcasted_iota(jnp.int32, (chunk, chunk), 0)
    cols = lax.broadcasted_iota(jnp.int32, (chunk, chunk), 1)
    tril = (rows >= cols).astype(F32)
    t_idx = lax.broadcasted_iota(jnp.int32, (sub, sub, 1), 0)
    s_idx = lax.broadcasted_iota(jnp.int32, (sub, sub, 1), 1)
    causal = t_idx >= s_idx
    ones = jnp.ones((LANES, LANES), BF16)
    n_sub = chunk // sub

    for h in range(GLA_HEADS):
        ks = slice(h * GLA_DK, (h + 1) * GLA_DK)
        vs = slice(h * GLA_DV, (h + 1) * GLA_DV)
        qh = q_ref[0, :, ks]
        kh = k_ref[0, :, ks]
        vh = v_ref[0, :, vs]
        b = _dot(tril, g_ref[0, :, ks], precision=HIGHEST)
        b_last = b[chunk - 1:chunk, :]
        st = st_ref[h]
        o = _dot_nt((qh * jnp.exp(b)).astype(BF16), st.astype(BF16))
        parts = []
        for i in range(n_sub):
            bt = b[i * sub:(i + 1) * sub]
            qi = qh[i * sub:(i + 1) * sub]
            acc0 = jnp.zeros((sub, LANES), F32)
            acc1 = jnp.zeros((sub, LANES), F32)
            for jj in range(i + 1):
                bs = b[jj * sub:(jj + 1) * sub]
                kj = kh[jj * sub:(jj + 1) * sub]
                vj = vh[jj * sub:(jj + 1) * sub]
                diff = bt[:, None, :] - bs[None, :, :]
                if jj == i:
                    diff = jnp.where(causal, diff, -jnp.inf)
                a = (qi[:, None, :] * kj[None, :, :]) * jnp.exp(diff)
                att = _dot(a.reshape(sub * sub, GLA_DK).astype(BF16), ones).reshape(sub, sub, LANES)
                acc0 = acc0 + jnp.sum(att * vj[None, :, :LANES], axis=1)
                acc1 = acc1 + jnp.sum(att * vj[None, :, LANES:], axis=1)
            parts.append(jnp.concatenate([acc0, acc1], axis=1))
        o_ref[0, :, vs] = o + jnp.concatenate(parts, axis=0)
        kd = (kh * jnp.exp(b_last - b)).astype(BF16)
        st_ref[h] = st * jnp.exp(b_last) + _dot_tn(vh.astype(BF16), kd)

    @pl.when(j == pl.num_programs(1) - 1)
    def _():
        sfin_ref[0] = st_ref[...]


def _gla_scan(q, k, v, lg, s0t, chunk, sub):
    b, s, gk = q.shape
    gv = v.shape[-1]
    seq = lambda w: pl.BlockSpec((1, chunk, w), lambda bi, j: (bi, j, 0))
    st_spec = pl.BlockSpec((1, GLA_HEADS, GLA_DV, GLA_DK), lambda bi, j: (bi, 0, 0, 0))
    return pl.pallas_call(
        functools.partial(_gla_scan_kernel, chunk=chunk, sub=sub),
        grid=(b, s // chunk),
        in_specs=[seq(gk), seq(gk), seq(gv), seq(gk), st_spec],
        out_specs=[seq(gv), st_spec],
        out_shape=[jax.ShapeDtypeStruct((b, s, gv), F32),
                   jax.ShapeDtypeStruct((b, GLA_HEADS, GLA_DV, GLA_DK), F32)],
        scratch_shapes=[pltpu.VMEM((GLA_HEADS, GLA_DV, GLA_DK), F32)],
        compiler_params=_params(2),
        name="gla_scan",
    )(q, k, v, lg, s0t)


def _post_mix_kernel(*refs, gla):
    if gla:
        o_ref, r_ref, go_ref, wo_ref, x_ref, gf_ref, x1_ref, xnt_ref = refs
        parts = []
        for h in range(GLA_HEADS):
            vs = slice(h * GLA_DV, (h + 1) * GLA_DV)
            parts.append(_rms(o_ref[:, vs], go_ref[...]) * r_ref[:, vs])
        a = jnp.concatenate(parts, axis=1)
    else:
        o_ref, wo_ref, x_ref, gf_ref, x1_ref, xnt_ref = refs
        a = o_ref[...]
    x1 = x_ref[...] + _dot(a.astype(BF16), wo_ref[...])
    x1_ref[...] = x1
    xnt_ref[...] = _rms(x1, gf_ref[...]).T.astype(BF16)


def _post_mix(o, r, go, wo, x, gf, tm):
    n, d = x.shape
    tok = pl.BlockSpec((tm, d), lambda i: (i, 0))
    gla = r is not None
    if gla:
        args = (o, r, go, wo, x, gf)
        in_specs = [tok, tok, _const_spec((1, GLA_DV)), _const_spec((d, d)), tok, _const_spec((1, d))]
    else:
        args = (o, wo, x, gf)
        in_specs = [tok, _const_spec((d, d)), tok, _const_spec((1, d))]
    return pl.pallas_call(
        functools.partial(_post_mix_kernel, gla=gla),
        grid=(n // tm,),
        in_specs=in_specs,
        out_specs=[tok, pl.BlockSpec((d, tm), lambda i: (0, i))],
        out_shape=[jax.ShapeDtypeStruct((n, d), F32), jax.ShapeDtypeStruct((d, n), BF16)],
        compiler_params=_params(1),
        name="post_mix_gla" if gla else "post_mix_fox",
    )(*args)


_PEER_CAND_ROWS = 80


def _peer_candidates(v1, v2):
    r8 = lax.broadcasted_iota(jnp.int32, (8, 1), 0)
    pieces = []
    for a in range(8):
        z = v1[a:a + 1] + v2[0:8]
        pieces.append(z if a == 0 else jnp.where(r8 < PEER_TOPK // (a + 1), z, jnp.nan))
    pieces.append(v1[0:1] + v2[8:16])
    pieces.append(v1[8:16] + v2[0:1])
    return jnp.concatenate(pieces, axis=0)


def _peer_cand_valid_rows():
    rows = []
    for a in range(8):
        rows += [8 * a + b for b in range(min(8, PEER_TOPK // (a + 1)))]
    return rows + list(range(64, 80))


def _peer_route_kernel(xnt_ref, wpqt_ref, keys_ref, s1_ref, s2_ref, tau_ref, top_ref):
    xnt = xnt_ref[...]
    valid = jnp.zeros((_PEER_CAND_ROWS, 1), jnp.bool_)
    ridx = lax.broadcasted_iota(jnp.int32, (_PEER_CAND_ROWS, 1), 0)
    for r in _peer_cand_valid_rows():
        valid = valid | (ridx == r)

    def head(p, carry):
        s = []
        for half in range(2):
            row = pl.multiple_of((2 * p + half) * PEER_NKEYS, PEER_NKEYS)
            qt = _dot(wpqt_ref[pl.ds(row, PEER_NKEYS), :], xnt).astype(BF16)
            sc = _dot(keys_ref[2 * p + half], qt)
            s.append(sc)
            cur = sc
            for rnk in range(PEER_TOPK):
                mx = jnp.max(cur, axis=0, keepdims=True)
                top_ref[half, rnk:rnk + 1, :] = mx
                cur = jnp.where(cur == mx, -jnp.inf, cur)
        v1 = top_ref[0]
        v2 = top_ref[1]
        m = v1[0:1] + v2[0:1]
        v1m = v1 - m
        z = jnp.where(valid, _peer_candidates(v1m, v2), -jnp.inf)
        tau = jnp.full_like(m, -jnp.inf)
        for r in _peer_cand_valid_rows():
            zc = z[r:r + 1]
            cnt = jnp.sum((z >= zc).astype(F32), axis=0, keepdims=True)
            tau = jnp.maximum(tau, jnp.where(cnt >= float(PEER_TOPK), zc, -jnp.inf))
        sel = z >= tau
        log_z = jnp.log(jnp.sum(jnp.where(sel, jnp.exp(z), 0.0), axis=0, keepdims=True))
        z2 = _peer_candidates(v1m - log_z, v2)
        tau2 = jnp.min(jnp.where(sel, z2, jnp.inf), axis=0, keepdims=True)
        row = pl.multiple_of(p * PEER_NKEYS, PEER_NKEYS)
        s1_ref[pl.ds(row, PEER_NKEYS), :] = (s[0] - m) - log_z
        s2_ref[pl.ds(row, PEER_NKEYS), :] = s[1]
        tau_ref[pl.ds(p, 1), :] = tau2
        return carry

    lax.fori_loop(0, PEER_HEADS, head, 0)


def _peer_route(xnt, wpqt, keys, tn):
    d, n = xnt.shape
    rows = PEER_HEADS * PEER_NKEYS
    return pl.pallas_call(
        _peer_route_kernel,
        grid=(n // tn,),
        in_specs=[pl.BlockSpec((d, tn), lambda i: (0, i)), _const_spec((2 * rows, d)),
                  _const_spec((2 * PEER_HEADS, PEER_NKEYS, PEER_NKEYS))],
        out_specs=[pl.BlockSpec((rows, tn), lambda i: (0, i)), pl.BlockSpec((rows, tn), lambda i: (0, i)),
                   pl.BlockSpec((PEER_HEADS, tn), lambda i: (0, i))],
        out_shape=[jax.ShapeDtypeStruct((rows, n), F32), jax.ShapeDtypeStruct((rows, n), F32),
                   jax.ShapeDtypeStruct((PEER_HEADS, n), F32)],
        scratch_shapes=[pltpu.VMEM((2, PEER_TOPK, tn), F32)],
        compiler_params=_params(1),
        name="peer_route",
    )(xnt, wpqt, keys)


def _peer_dense_kernel(xnt_ref, u_ref, vt_ref, s1_ref, s2_ref, tau_ref, x1_ref, gfin_ref, out_ref,
                       acc_ref, act_ref, p_ref, *, te, tn, lc, final_norm):
    e = pl.program_id(1)

    @pl.when(e == 0)
    def _():
        acc_ref[...] = jnp.zeros_like(acc_ref)

    act_ref[...] = _dot(u_ref[...], xnt_ref[...])
    n_i1 = te // PEER_NKEYS
    n_lc = tn // lc

    def body(it, carry):
        il = it // n_lc
        col = pl.multiple_of((it % n_lc) * lc, lc)
        row = pl.multiple_of(il * PEER_NKEYS, PEER_NKEYS)
        i1 = e * n_i1 + il
        w = jnp.zeros((PEER_NKEYS, lc), F32)
        for p in range(PEER_HEADS):
            z = (s2_ref[p * PEER_NKEYS:(p + 1) * PEER_NKEYS, pl.ds(col, lc)]
                 + s1_ref[pl.ds(p * PEER_NKEYS + i1, 1), pl.ds(col, lc)])
            w = w + jnp.where(z >= tau_ref[p:p + 1, pl.ds(col, lc)], jnp.exp(z), 0.0)
        a = act_ref[pl.ds(row, PEER_NKEYS), pl.ds(col, lc)]
        gelu = 0.5 * a * (1.0 + lax.erf(a * (2.0 ** -0.5)))
        p_ref[pl.ds(row, PEER_NKEYS), pl.ds(col, lc)] = (w * gelu).astype(BF16)
        return carry

    lax.fori_loop(0, n_i1 * n_lc, body, 0)
    acc_ref[...] += _dot(vt_ref[...], p_ref[...])

    @pl.when(e == pl.num_programs(1) - 1)
    def _():
        y = x1_ref[...] + acc_ref[...].T
        if final_norm:
            y = _rms(y, gfin_ref[...])
        out_ref[...] = y


def _peer_dense(xnt, u, vt, s1, s2, tau, x1, gfin, tn, te, final_norm):
    d, n = xnt.shape
    rows = PEER_HEADS * PEER_NKEYS
    lc = min(tn, 256)
    tokT = lambda r: pl.BlockSpec((r, tn), lambda i, e: (0, i))
    tok = pl.BlockSpec((tn, d), lambda i, e: (i, 0))
    return pl.pallas_call(
        functools.partial(_peer_dense_kernel, te=te, tn=tn, lc=lc, final_norm=final_norm),
        grid=(n // tn, PEER_N_EXPERTS // te),
        in_specs=[tokT(d), pl.BlockSpec((te, d), lambda i, e: (e, 0)), pl.BlockSpec((d, te), lambda i, e: (0, e)),
                  tokT(rows), tokT(rows), tokT(PEER_HEADS), tok, pl.BlockSpec((1, d), lambda i, e: (0, 0))],
        out_specs=tok,
        out_shape=jax.ShapeDtypeStruct((n, d), F32),
        scratch_shapes=[pltpu.VMEM((d, tn), F32), pltpu.VMEM((te, tn), F32), pltpu.VMEM((te, tn), BF16)],
        compiler_params=_params(2),
        name="peer_dense",
    )(xnt, u, vt, s1, s2, tau, x1, gfin)


def _fox_proj_kernel(x_ref, g_ref, wq_ref, wk_ref, wv_ref, wf_ref, bf_ref,
                     qb_ref, k_ref, v_ref, kb_ref, vb_ref, lf_ref, lft_ref):
    h = _rms(x_ref[...], g_ref[...]).astype(BF16)
    qb_ref[...] = (_dot(h, wq_ref[...]) * (FOX_HD ** -0.5)).astype(BF16)
    k = _dot(h, wk_ref[...])
    k_ref[...] = k
    kb_ref[...] = k.astype(BF16)
    v = _dot(h, wv_ref[...])
    v_ref[...] = v
    vb_ref[...] = v.astype(BF16)
    lf = _log_sigmoid(_dot(h, wf_ref[...]) + bf_ref[...])
    lf_ref[...] = lf[:, :FOX_HEADS]
    lft_ref[...] = lf.T[:FOX_HEADS, :]


def _fox_proj(x, g, wq, wk, wv, wf, bf, tm):
    n, d = x.shape
    tok = pl.BlockSpec((tm, d), lambda i: (i, 0))
    return pl.pallas_call(
        _fox_proj_kernel,
        grid=(n // tm,),
        in_specs=[tok, _const_spec((1, d)), _const_spec((d, d)), _const_spec((d, d)), _const_spec((d, d)),
                  _const_spec((d, LANES)), _const_spec((1, LANES))],
        out_specs=[tok, tok, tok, tok, tok, pl.BlockSpec((tm, FOX_HEADS), lambda i: (i, 0)),
                   pl.BlockSpec((FOX_HEADS, tm), lambda i: (0, i))],
        out_shape=[jax.ShapeDtypeStruct((n, d), BF16), jax.ShapeDtypeStruct((n, d), F32),
                   jax.ShapeDtypeStruct((n, d), F32), jax.ShapeDtypeStruct((n, d), BF16),
                   jax.ShapeDtypeStruct((n, d), BF16), jax.ShapeDtypeStruct((n, FOX_HEADS), F32),
                   jax.ShapeDtypeStruct((FOX_HEADS, n), F32)],
        compiler_params=_params(1),
        name="fox_proj",
    )(x, g, wq, wk, wv, wf, bf)


def _fox_cumsum_kernel(lf_ref, lft_ref, c_ref, ct_ref, *, blk):
    n_blk = lf_ref.shape[0] // blk
    rows = lax.broadcasted_iota(jnp.int32, (blk, blk), 0)
    cols = lax.broadcasted_iota(jnp.int32, (blk, blk), 1)
    lower = (rows >= cols).astype(F32)
    upper = (rows <= cols).astype(F32)
    ones = jnp.ones((blk, blk), F32)

    def body(i, carry):
        c_row, c_col = carry
        r0 = pl.multiple_of(i * blk, blk)
        c_blk = _dot(lower, lf_ref[pl.ds(r0, blk), :], precision=HIGHEST) + c_row
        c_ref[pl.ds(r0, blk), :] = c_blk
        lft = lft_ref[:, pl.ds(r0, blk)]
        ct_ref[:, pl.ds(r0, blk)] = _dot(lft, upper, precision=HIGHEST) + c_col
        return c_blk[blk - 1:blk, :], c_col + _dot(lft, ones, precision=HIGHEST)

    lax.fori_loop(0, n_blk, body, (jnp.zeros((1, FOX_HEADS), F32), jnp.zeros((FOX_HEADS, blk), F32)))


def _fox_cumsum(lf, lft, batch):
    n = lf.shape[0]
    s = n // batch
    return pl.pallas_call(
        functools.partial(_fox_cumsum_kernel, blk=LANES),
        grid=(batch,),
        in_specs=[pl.BlockSpec((s, FOX_HEADS), lambda b: (b, 0)), pl.BlockSpec((FOX_HEADS, s), lambda b: (0, b))],
        out_specs=[pl.BlockSpec((s, FOX_HEADS), lambda b: (b, 0)), pl.BlockSpec((FOX_HEADS, s), lambda b: (0, b))],
        out_shape=[jax.ShapeDtypeStruct((n, FOX_HEADS), F32), jax.ShapeDtypeStruct((FOX_HEADS, n), F32)],
        compiler_params=_params(1),
        name="fox_cumsum",
    )(lf, lft)


def _fox_flash_kernel(qi_ref, ki_ref, q_ref, k_ref, v_ref, c_ref, ct_ref, o_ref,
                      qm_ref, ctb_ref, m_ref, l_ref, acc_ref, *, tq, tk):
    t = pl.program_id(2)
    qi = qi_ref[t]
    ki = ki_ref[t]
    lane = lax.broadcasted_iota(jnp.int32, (tq, LANES), 1)

    @pl.when(ki == 0)
    def _():
        q = q_ref[...]
        c = c_ref[0]
        for hh in range(2):
            qm_ref[hh] = jnp.where((lane >= hh * FOX_HD) & (lane < (hh + 1) * FOX_HD), q, jnp.zeros_like(q))
            ctb_ref[hh] = jnp.broadcast_to(c[:, hh:hh + 1], (tq, tk))
        m_ref[...] = jnp.full_like(m_ref, -jnp.inf)
        l_ref[...] = jnp.zeros_like(l_ref)
        acc_ref[...] = jnp.zeros_like(acc_ref)

    def step(masked):
        k = k_ref[...]
        v = v_ref[...]
        ct = ct_ref[0]
        if masked:
            keep = (lax.broadcasted_iota(jnp.int32, (tq, tk), 0) >= lax.broadcasted_iota(jnp.int32, (tq, tk), 1))
        for hh in range(2):
            s = _dot_nt(qm_ref[hh], k) + (ctb_ref[hh] - ct[hh:hh + 1, :])
            if masked:
                s = jnp.where(keep, s, -jnp.inf)
            m_prev = m_ref[hh][:, :1]
            m_new = jnp.maximum(m_prev, jnp.max(s, axis=1, keepdims=True))
            alpha = jnp.exp(m_prev - m_new)
            p = jnp.exp(s - m_new)
            l_ref[hh] = jnp.broadcast_to(alpha * l_ref[hh][:, :1] + jnp.sum(p, axis=1, keepdims=True), (tq, LANES))
            m_ref[hh] = jnp.broadcast_to(m_new, (tq, LANES))
            acc_ref[hh] = acc_ref[hh] * alpha + _dot(p.astype(BF16), v)

    @pl.when(ki < qi)
    def _():
        step(False)

    @pl.when(ki == qi)
    def _():
        step(True)
        o0 = acc_ref[0] / l_ref[0]
        o1 = acc_ref[1] / l_ref[1]
        o_ref[...] = jnp.where(lane < FOX_HD, o0, o1).astype(o_ref.dtype)


def _fox_flash(qb, kb, vb, c3, ct3, batch, tq):
    n, d = qb.shape
    s = n // batch
    nq = s // tq
    tri = [(qi, ki) for qi in range(nq) for ki in range(qi + 1)]
    qi_of = jnp.array([a for a, _ in tri], jnp.int32)
    ki_of = jnp.array([b for _, b in tri], jnp.int32)
    q_map = lambda b, pr, t, qi, ki: (b * nq + qi[t], pr)
    k_map = lambda b, pr, t, qi, ki: (b * nq + ki[t], pr)
    grid_spec = pltpu.PrefetchScalarGridSpec(
        num_scalar_prefetch=2,
        grid=(batch, FOX_PAIRS, len(tri)),
        in_specs=[pl.BlockSpec((tq, LANES), q_map), pl.BlockSpec((tq, LANES), k_map), pl.BlockSpec((tq, LANES), k_map),
                  pl.BlockSpec((1, tq, 2), lambda b, pr, t, qi, ki: (pr, b * nq + qi[t], 0)),
                  pl.BlockSpec((1, 2, tq), lambda b, pr, t, qi, ki: (pr, 0, b * nq + ki[t]))],
        out_specs=pl.BlockSpec((tq, LANES), q_map),
        scratch_shapes=[pltpu.VMEM((2, tq, LANES), BF16), pltpu.VMEM((2, tq, tq), F32), pltpu.VMEM((2, tq, LANES), F32),
                        pltpu.VMEM((2, tq, LANES), F32), pltpu.VMEM((2, tq, LANES), F32)],
    )
    return pl.pallas_call(
        functools.partial(_fox_flash_kernel, tq=tq, tk=tq),
        grid_spec=grid_spec,
        out_shape=jax.ShapeDtypeStruct((n, d), BF16),
        compiler_params=_params(3),
        name="fox_flash",
    )(qi_of, ki_of, qb, kb, vb, c3, ct3)


def _fox_sample_kernel(pt_ref, q_ref, lfc_ref, kn_ref, vn_ref, lfn_ref, *rest, pages_per_step, t_new):
    k_refs = rest[:pages_per_step]
    v_refs = rest[pages_per_step:2 * pages_per_step]
    lf_refs = rest[2 * pages_per_step:3 * pages_per_step]
    o_ref, qbd_ref, cn_ref, m_ref, l_ref, acc_ref, tot_ref = rest[3 * pages_per_step:]
    step = pl.program_id(1)
    n_rows = t_new * FOX_HEADS
    page = k_refs[0].shape[1]
    rows_i = lax.broadcasted_iota(jnp.int32, (page, page), 0)
    cols_i = lax.broadcasted_iota(jnp.int32, (page, page), 1)

    def attend(k, v, bias):
        s = _dot_nt(qbd_ref[...], k.astype(BF16)) + bias
        m_prev = m_ref[:, :1]
        m_new = jnp.maximum(m_prev, jnp.max(s, axis=1, keepdims=True))
        alpha = jnp.exp(m_prev - m_new)
        p = jnp.exp(s - m_new)
        l_ref[...] = jnp.broadcast_to(alpha * l_ref[:, :1] + jnp.sum(p, axis=1, keepdims=True), l_ref.shape)
        m_ref[...] = jnp.broadcast_to(m_new, m_ref.shape)
        acc_ref[...] = acc_ref[...] * alpha + _dot(p.astype(BF16), v.astype(BF16))

    @pl.when(step == 0)
    def _():
        head_of_lane = lax.broadcasted_iota(jnp.int32, (FOX_HEADS, D_MODEL), 1) // FOX_HD
        head_of_row = lax.broadcasted_iota(jnp.int32, (FOX_HEADS, D_MODEL), 0)
        cn = jnp.zeros((FOX_HEADS, 1), F32)
        for t in range(t_new):
            qt = jnp.broadcast_to(q_ref[0, t:t + 1, :], (FOX_HEADS, D_MODEL))
            qbd_ref[t * FOX_HEADS:(t + 1) * FOX_HEADS, :] = jnp.where(head_of_lane == head_of_row, qt, 0.0).astype(BF16)
            cn = cn + lfc_ref[0, t * FOX_HEADS:(t + 1) * FOX_HEADS, :]
            cn_ref[t * FOX_HEADS:(t + 1) * FOX_HEADS, :] = jnp.broadcast_to(cn, (FOX_HEADS, LANES))
        m_ref[...] = jnp.full_like(m_ref, -jnp.inf)
        l_ref[...] = jnp.zeros_like(l_ref)
        acc_ref[...] = jnp.zeros_like(acc_ref)
        tot_ref[...] = jnp.zeros_like(tot_ref)
        cnt = _dot(lfn_ref[0], (rows_i <= cols_i).astype(F32), precision=HIGHEST)
        bias = cn_ref[...] - jnp.concatenate([cnt] * t_new, axis=0)
        t_of_row = lax.broadcasted_iota(jnp.int32, (n_rows, page), 0) // FOX_HEADS
        slot = lax.broadcasted_iota(jnp.int32, (n_rows, page), 1)
        attend(kn_ref[0], vn_ref[0], jnp.where(slot <= t_of_row, bias, -jnp.inf))

    strict = (rows_i > cols_i).astype(F32)
    ones = jnp.ones((page, page), F32)
    for i in range(pages_per_step):
        lft = lf_refs[i][0]
        r = _dot(lft, strict, precision=HIGHEST) + tot_ref[...]
        tot_ref[...] = tot_ref[...] + _dot(lft, ones, precision=HIGHEST)
        attend(k_refs[i][0], v_refs[i][0], cn_ref[...] + jnp.concatenate([r] * t_new, axis=0))

    @pl.when(step == pl.num_programs(1) - 1)
    def _():
        own = (lax.broadcasted_iota(jnp.int32, (n_rows, D_MODEL), 1) // FOX_HD
               == lax.broadcasted_iota(jnp.int32, (n_rows, D_MODEL), 0) % FOX_HEADS)
        o = jnp.where(own, acc_ref[...] / l_ref[:, :1], 0.0)
        o_ref[0] = jnp.sum(o.reshape(t_new, FOX_HEADS, D_MODEL), axis=1)


def _fox_sample(q, lf_col, k_new, v_new, lft_new, cache_k, cache_v, cache_lft, page_table, pages_per_step):
    db, t_new, d = q.shape
    n_pages = page_table.shape[1]
    page = cache_k.shape[1]
    n_rows = t_new * FOX_HEADS
    n_steps = n_pages // pages_per_step

    def page_map(i):
        return lambda b, s, pt: (pt[b, n_pages - 1 - (s * pages_per_step + i)], 0, 0)

    per_b = lambda shape: pl.BlockSpec((1,) + shape, lambda b, s, pt: (b, 0, 0))
    in_specs = [per_b((t_new, d)), per_b((n_rows, 1)), per_b((page, d)), per_b((page, d)), per_b((FOX_HEADS, page))]
    in_specs += [pl.BlockSpec((1, page, d), page_map(i)) for i in range(pages_per_step)]
    in_specs += [pl.BlockSpec((1, page, d), page_map(i)) for i in range(pages_per_step)]
    in_specs += [pl.BlockSpec((1, FOX_HEADS, page), page_map(i)) for i in range(pages_per_step)]
    grid_spec = pltpu.PrefetchScalarGridSpec(
        num_scalar_prefetch=1,
        grid=(db, n_steps),
        in_specs=in_specs,
        out_specs=per_b((t_new, d)),
        scratch_shapes=[pltpu.VMEM((n_rows, d), BF16), pltpu.VMEM((n_rows, LANES), F32), pltpu.VMEM((n_rows, LANES), F32),
                        pltpu.VMEM((n_rows, LANES), F32), pltpu.VMEM((n_rows, d), F32), pltpu.VMEM((FOX_HEADS, page), F32)],
    )
    return pl.pallas_call(
        functools.partial(_fox_sample_kernel, pages_per_step=pages_per_step, t_new=t_new),
        grid_spec=grid_spec,
        out_shape=jax.ShapeDtypeStruct((db, t_new, d), F32),
        compiler_params=_params(2),
        name="fox_sample",
    )(page_table, q, lf_col, k_new, v_new, lft_new,
      *([cache_k] * pages_per_step), *([cache_v] * pages_per_step), *([cache_lft] * pages_per_step))


def _tile(n, pref):
    return pref if n % pref == 0 else n


def _peer(x1, xnt, wpqt, keys, u, vt, gfin, final_norm):
    n = x1.shape[0]
    s1, s2, tau = _peer_route(xnt, wpqt, keys, _tile(n, 256))
    return _peer_dense(xnt, u, vt, s1, s2, tau, x1, gfin, _tile(n, 1024), 512, final_norm)


def kernel(x_prompt, x_sample, state_gla, cache_k, cache_v, cache_logf, page_table, g_mix, g_ffn, g_final, gla_w_q, gla_w_k, gla_w_v, gla_w_g1, gla_w_g2, gla_b_g, gla_w_r, gla_b_r, gla_g_o, gla_w_o, fox_w_q, fox_w_k, fox_w_v, fox_w_f, fox_b_f, fox_w_o, peer_w_q, peer_keys, peer_u, peer_v):
    batch, seq, d = x_prompt.shape
    db, t_new, _ = x_sample.shape
    n_p, n_s = batch * seq, db * t_new
    xp = x_prompt.reshape(n_p, d)
    xs = x_sample.reshape(n_s, d)
    row = lambda a: a.reshape(1, -1).astype(F32)
    gfin = row(g_final)

    peer = []
    for i in range(peer_w_q.shape[0]):
        peer.append((peer_w_q[i].T.astype(BF16),
                     peer_keys[i].reshape(2 * PEER_HEADS, PEER_NKEYS, PEER_NKEYS).astype(BF16),
                     peer_u[i].astype(BF16), peer_v[i].T.astype(BF16)))

    rank = gla_w_g1.shape[-1]
    wg1 = jnp.pad(gla_w_g1[0], ((0, 0), (0, LANES - rank))).astype(BF16)
    wg2 = jnp.pad(gla_w_g2[0], ((0, LANES - rank), (0, 0))).astype(BF16)
    gla_w = (row(g_mix[0]), gla_w_q[0].astype(BF16), gla_w_k[0].astype(BF16), gla_w_v[0].astype(BF16),
             gla_w_r[0].astype(BF16), wg1, wg2, row(gla_b_g[0]), row(gla_b_r[0]))
    wo0 = gla_w_o[0].astype(BF16)

    def gla_layer(x, b, s, s0t, chunk, sub, pad_to):
        n = x.shape[0]
        q, k, v, lg, r = _gla_proj(x, *gla_w, _tile(n, 512))
        shp = lambda a: a.reshape(b, s, a.shape[-1])
        q, k, v, lg = shp(q), shp(k), shp(v), shp(lg)
        if pad_to > s:
            padseq = lambda a: jnp.pad(a, ((0, 0), (0, pad_to - s), (0, 0)))
            q, k, v, lg = padseq(q), padseq(k), padseq(v), padseq(lg)
        o, sfin_t = _gla_scan(q, k, v, lg, s0t, chunk, sub)
        o = o[:, :s].reshape(n, -1)
        x1, xnt = _post_mix(o, r, row(gla_g_o[0]), wo0, x, row(g_ffn[0]), _tile(n, 512))
        return _peer(x1, xnt, *peer[0], gfin, False), jnp.swapaxes(sfin_t, -1, -2)

    s0p = jnp.zeros((batch, GLA_HEADS, GLA_DV, GLA_DK), F32)
    xp, gla_sp = gla_layer(xp, batch, seq, s0p, GLA_CHUNK, GLA_SUB, seq)
    s0s = jnp.swapaxes(state_gla[0], -1, -2).astype(F32)
    xs, gla_ss = gla_layer(xs, db, t_new, s0s, 8, 8, 8)

    wf = jnp.pad(fox_w_f[0], ((0, 0), (0, LANES - FOX_HEADS))).astype(BF16)
    bf = jnp.pad(fox_b_f[0], (0, LANES - FOX_HEADS)).reshape(1, LANES).astype(F32)
    fox_w = (row(g_mix[1]), fox_w_q[0].astype(BF16), fox_w_k[0].astype(BF16), fox_w_v[0].astype(BF16), wf, bf)
    wo1 = fox_w_o[0].astype(BF16)

    qb, kp, vp, kb, vb, lfp, lfp_t = _fox_proj(xp, *fox_w, _tile(n_p, 512))
    c, c_t = _fox_cumsum(lfp, lfp_t, batch)
    c3 = c.reshape(n_p, FOX_PAIRS, 2).transpose(1, 0, 2)
    ct3 = c_t.reshape(FOX_PAIRS, 2, n_p)
    op = _fox_flash(qb, kb, vb, c3, ct3, batch, _tile(seq, 512))
    x1, xnt = _post_mix(op, None, None, wo1, xp, row(g_ffn[1]), _tile(n_p, 512))
    yp = _peer(x1, xnt, *peer[1], gfin, True)

    qs, ks, vs, _, _, lfs, lfs_t = _fox_proj(xs, *fox_w, _tile(n_s, 512))
    page = cache_k.shape[2]
    n_pool = cache_k.shape[1]
    ck = cache_k[0].reshape(n_pool, page, d)
    cv = cache_v[0].reshape(n_pool, page, d)
    clft = jnp.swapaxes(cache_logf[0], -1, -2)
    qs_f32 = qs.astype(F32)
    k_new = jnp.pad(ks.reshape(db, t_new, d), ((0, 0), (0, page - t_new), (0, 0)))
    v_new = jnp.pad(vs.reshape(db, t_new, d), ((0, 0), (0, page - t_new), (0, 0)))
    lft_new = jnp.pad(lfs_t.reshape(FOX_HEADS, db, t_new).transpose(1, 0, 2), ((0, 0), (0, 0), (0, page - t_new)))
    lf_col = lfs.reshape(db, t_new * FOX_HEADS, 1)
    os_ = _fox_sample(qs_f32.reshape(db, t_new, d), lf_col, k_new, v_new, lft_new, ck, cv, clft, page_table, 8)
    x1, xnt = _post_mix(os_.reshape(n_s, d), None, None, wo1, xs, row(g_ffn[1]), _tile(n_s, 512))
    ys = _peer(x1, xnt, *peer[1], gfin, True)

    return (yp.reshape(batch, seq, d), ys.reshape(db, t_new, d), gla_sp[None], gla_ss[None],
            kp.reshape(1, batch, seq, FOX_HEADS, FOX_HD), vp.reshape(1, batch, seq, FOX_HEADS, FOX_HD),
            lfp.reshape(1, batch, seq, FOX_HEADS),
            ks.reshape(1, db, t_new, FOX_HEADS, FOX_HD), vs.reshape(1, db, t_new, FOX_HEADS, FOX_HD),
            lfs.reshape(1, db, t_new, FOX_HEADS))
```

```python
import functools

import numpy as np
import jax
import jax.numpy as jnp
from jax import lax
from jax.experimental import pallas as pl
from jax.experimental.pallas import tpu as pltpu

F32 = jnp.float32
BF16 = jnp.bfloat16
HIGHEST = lax.Precision.HIGHEST

D_MODEL = 1024
GLA_HEADS = 4
GLA_DK = 128
GLA_DV = 256
GLA_TAU = 16.0
GLA_CHUNK = 64
GLA_SUB = 16
FOX_HEADS = 16
FOX_HD = 64
FOX_PAIRS = FOX_HEADS * FOX_HD // 128
PEER_HEADS = 8
PEER_NKEYS = 128
PEER_TOPK = 16
PEER_N_EXPERTS = PEER_NKEYS * PEER_NKEYS
RMS_EPS = 1e-6
LOG2E = 1.4426950408889634
LANES = 128
VMEM_LIMIT = 56 * 1024 * 1024


def _params(n_axes, vmem=VMEM_LIMIT):
    return pltpu.CompilerParams(dimension_semantics=("arbitrary",) * n_axes, vmem_limit_bytes=vmem)


def _rms(x, g):
    return x * lax.rsqrt(jnp.mean(x * x, axis=-1, keepdims=True) + RMS_EPS) * g


def _dot(a, b, precision=None):
    return jnp.dot(a, b, preferred_element_type=F32, precision=precision)


def _dot_nt(a, b):
    return lax.dot_general(a, b, (((1,), (1,)), ((), ())), preferred_element_type=F32)


def _dot_tn(a, b):
    return lax.dot_general(a, b, (((0,), (0,)), ((), ())), preferred_element_type=F32)


def _log_sigmoid(z):
    return jnp.minimum(z, 0.0) - jnp.log1p(jnp.exp(-jnp.abs(z)))


def _const_spec(shape):
    return pl.BlockSpec(shape, lambda *_: (0,) * len(shape))


def _gla_proj_kernel(x_ref, g_ref, wq_ref, wk_ref, wv_ref, wr_ref, wg1_ref, wg2_ref, bg_ref, br_ref,
                     q_ref, k_ref, v_ref, lg_ref, r_ref):
    h = _rms(x_ref[...], g_ref[...]).astype(BF16)
    q_ref[...] = _dot(h, wq_ref[...]) * (GLA_DK ** -0.5)
    k_ref[...] = _dot(h, wk_ref[...])
    v_ref[...] = _dot(h, wv_ref[...])
    u = _dot(h, wr_ref[...]) + br_ref[...]
    r_ref[...] = u / (1.0 + jnp.exp(-u))
    t = _dot(h, wg1_ref[...]).astype(BF16)
    z = _dot(t, wg2_ref[...]) + bg_ref[...]
    lg_ref[...] = _log_sigmoid(z) * (1.0 / GLA_TAU)


def _gla_proj(x, g, wq, wk, wv, wr, wg1, wg2, bg, br, tm):
    n, d = x.shape
    gk = GLA_HEADS * GLA_DK
    gv = GLA_HEADS * GLA_DV
    tok = lambda w: pl.BlockSpec((tm, w), lambda i: (i, 0))
    return pl.pallas_call(
        _gla_proj_kernel,
        grid=(n // tm,),
        in_specs=[tok(d), _const_spec((1, d)), _const_spec((d, gk)), _const_spec((d, gk)), _const_spec((d, gv)),
                  _const_spec((d, gv)), _const_spec((d, LANES)), _const_spec((LANES, gk)), _const_spec((1, gk)),
                  _const_spec((1, gv))],
        out_specs=[tok(gk), tok(gk), tok(gv), tok(gk), tok(gv)],
        out_shape=[jax.ShapeDtypeStruct((n, gk), F32), jax.ShapeDtypeStruct((n, gk), F32),
                   jax.ShapeDtypeStruct((n, gv), F32), jax.ShapeDtypeStruct((n, gk), F32),
                   jax.ShapeDtypeStruct((n, gv), F32)],
        compiler_params=_params(1),
        name="gla_proj",
    )(x, g, wq, wk, wv, wr, wg1, wg2, bg, br)


def _gla_scan_kernel(q_ref, k_ref, v_ref, g_ref, s0_ref, o_ref, sfin_ref, st_ref, *, chunk, sub):
    j = pl.program_id(1)

    @pl.when(j == 0)
    def _():
        st_ref[...] = s0_ref[0]

    rows = lax.broadcasted_iota(jnp.int32, (chunk, chunk), 0)
    cols = lax.broadcasted_iota(jnp.int32, (chunk, chunk), 1)
    tril = (rows >= cols).astype(F32)
    t_idx = lax.broadcasted_iota(jnp.int32, (sub, sub, 1), 0)
    s_idx = lax.broadcasted_iota(jnp.int32, (sub, sub, 1), 1)
    causal = t_idx >= s_idx
    ones = jnp.ones((LANES, LANES), BF16)
    n_sub = chunk // sub

    for h in range(GLA_HEADS):
        ks = slice(h * GLA_DK, (h + 1) * GLA_DK)
        vs = slice(h * GLA_DV, (h + 1) * GLA_DV)
        qh = q_ref[0, :, ks]
        kh = k_ref[0, :, ks]
        vh = v_ref[0, :, vs]
        b = _dot(tril, g_ref[0, :, ks], precision=HIGHEST)
        b_last = b[chunk - 1:chunk, :]
        st = st_ref[h]
        o = _dot_nt((qh * jnp.exp(b)).astype(BF16), st.astype(BF16))
        parts = []
        for i in range(n_sub):
            bt = b[i * sub:(i + 1) * sub]
            qi = qh[i * sub:(i + 1) * sub]
            acc0 = jnp.zeros((sub, LANES), F32)
            acc1 = jnp.zeros((sub, LANES), F32)
            for jj in range(i + 1):
                bs = b[jj * sub:(jj + 1) * sub]
                kj = kh[jj * sub:(jj + 1) * sub]
                vj = vh[jj * sub:(jj + 1) * sub]
                diff = bt[:, None, :] - bs[None, :, :]
                if jj == i:
                    diff = jnp.where(causal, diff, -jnp.inf)
                a = (qi[:, None, :] * kj[None, :, :]) * jnp.exp(diff)
                att = _dot(a.reshape(sub * sub, GLA_DK).astype(BF16), ones).reshape(sub, sub, LANES)
                acc0 = acc0 + jnp.sum(att * vj[None, :, :LANES], axis=1)
                acc1 = acc1 + jnp.sum(att * vj[None, :, LANES:], axis=1)
            parts.append(jnp.concatenate([acc0, acc1], axis=1))
        o_ref[0, :, vs] = o + jnp.concatenate(parts, axis=0)
        kd = (kh * jnp.exp(b_last - b)).astype(BF16)
        st_ref[h] = st * jnp.exp(b_last) + _dot_tn(vh.astype(BF16), kd)

    @pl.when(j == pl.num_programs(1) - 1)
    def _():
        sfin_ref[0] = st_ref[...]


def _gla_scan(q, k, v, lg, s0t, chunk, sub):
    b, s, gk = q.shape
    gv = v.shape[-1]
    seq = lambda w: pl.BlockSpec((1, chunk, w), lambda bi, j: (bi, j, 0))
    st_spec = pl.BlockSpec((1, GLA_HEADS, GLA_DV, GLA_DK), lambda bi, j: (bi, 0, 0, 0))
    return pl.pallas_call(
        functools.partial(_gla_scan_kernel, chunk=chunk, sub=sub),
        grid=(b, s // chunk),
        in_specs=[seq(gk), seq(gk), seq(gv), seq(gk), st_spec],
        out_specs=[seq(gv), st_spec],
        out_shape=[jax.ShapeDtypeStruct((b, s, gv), F32),
                   jax.ShapeDtypeStruct((b, GLA_HEADS, GLA_DV, GLA_DK), F32)],
        scratch_shapes=[pltpu.VMEM((GLA_HEADS, GLA_DV, GLA_DK), F32)],
        compiler_params=_params(2),
        name="gla_scan",
    )(q, k, v, lg, s0t)


def _post_mix_kernel(*refs, gla):
    if gla:
        o_ref, r_ref, go_ref, wo_ref, x_ref, gf_ref, x1_ref, xnt_ref = refs
        parts = []
        for h in range(GLA_HEADS):
            vs = slice(h * GLA_DV, (h + 1) * GLA_DV)
            parts.append(_rms(o_ref[:, vs], go_ref[...]) * r_ref[:, vs])
        a = jnp.concatenate(parts, axis=1)
    else:
        o_ref, wo_ref, x_ref, gf_ref, x1_ref, xnt_ref = refs
        a = o_ref[...].T
    x1 = x_ref[...] + _dot(a.astype(BF16), wo_ref[...])
    x1_ref[...] = x1
    xnt_ref[...] = _rms(x1, gf_ref[...]).T.astype(BF16)


def _post_mix(o, r, go, wo, x, gf, tm):
    n, d = x.shape
    tok = pl.BlockSpec((tm, d), lambda i: (i, 0))
    gla = r is not None
    if gla:
        args = (o, r, go, wo, x, gf)
        in_specs = [tok, tok, _const_spec((1, GLA_DV)), _const_spec((d, d)), tok, _const_spec((1, d))]
    else:
        args = (o, wo, x, gf)
        in_specs = [pl.BlockSpec((d, tm), lambda i: (0, i)), _const_spec((d, d)), tok, _const_spec((1, d))]
    return pl.pallas_call(
        functools.partial(_post_mix_kernel, gla=gla),
        grid=(n // tm,),
        in_specs=in_specs,
        out_specs=[tok, pl.BlockSpec((d, tm), lambda i: (0, i))],
        out_shape=[jax.ShapeDtypeStruct((n, d), F32), jax.ShapeDtypeStruct((d, n), BF16)],
        compiler_params=_params(1),
        name="post_mix_gla" if gla else "post_mix_fox",
    )(*args)


_PEER_CAND_ROWS = 80


def _peer_candidates(v1, v2):
    r8 = lax.broadcasted_iota(jnp.int32, (8, 1), 0)
    pieces = []
    for a in range(8):
        z = v1[a:a + 1] + v2[0:8]
        pieces.append(z if a == 0 else jnp.where(r8 < PEER_TOPK // (a + 1), z, jnp.nan))
    pieces.append(v1[0:1] + v2[8:16])
    pieces.append(v1[8:16] + v2[0:1])
    return jnp.concatenate(pieces, axis=0)


def _peer_cand_valid_rows():
    rows = []
    for a in range(8):
        rows += [8 * a + b for b in range(min(8, PEER_TOPK // (a + 1)))]
    return rows + list(range(64, 80))


def _peer_route_kernel(xnt_ref, wpqt_ref, keys_ref, s1_ref, s2_ref, tau_ref, top_ref):
    xnt = xnt_ref[...]
    valid = jnp.zeros((_PEER_CAND_ROWS, 1), jnp.bool_)
    ridx = lax.broadcasted_iota(jnp.int32, (_PEER_CAND_ROWS, 1), 0)
    for r in _peer_cand_valid_rows():
        valid = valid | (ridx == r)

    def head(p, carry):
        s = []
        for half in range(2):
            row = pl.multiple_of((2 * p + half) * PEER_NKEYS, PEER_NKEYS)
            qt = _dot(wpqt_ref[pl.ds(row, PEER_NKEYS), :], xnt).astype(BF16)
            sc = _dot(keys_ref[2 * p + half], qt)
            s.append(sc)
            cur = sc
            for rnk in range(PEER_TOPK):
                mx = jnp.max(cur, axis=0, keepdims=True)
                top_ref[half, rnk:rnk + 1, :] = mx
                cur = jnp.where(cur == mx, -jnp.inf, cur)
        v1 = top_ref[0]
        v2 = top_ref[1]
        m = v1[0:1] + v2[0:1]
        v1m = v1 - m
        z = jnp.where(valid, _peer_candidates(v1m, v2), -jnp.inf)
        tau = jnp.full_like(m, -jnp.inf)
        for r in _peer_cand_valid_rows():
            zc = z[r:r + 1]
            cnt = jnp.sum((z >= zc).astype(F32), axis=0, keepdims=True)
            tau = jnp.maximum(tau, jnp.where(cnt >= float(PEER_TOPK), zc, -jnp.inf))
        sel = z >= tau
        log_z = jnp.log(jnp.sum(jnp.where(sel, jnp.exp(z), 0.0), axis=0, keepdims=True))
        z2 = _peer_candidates((v1m - log_z) * LOG2E, v2 * LOG2E)
        tau2 = jnp.min(jnp.where(sel, z2, jnp.inf), axis=0, keepdims=True)
        row = pl.multiple_of(p * PEER_NKEYS, PEER_NKEYS)
        s1_ref[pl.ds(row, PEER_NKEYS), :] = ((s[0] - m) - log_z) * LOG2E
        s2_ref[pl.ds(row, PEER_NKEYS), :] = s[1] * LOG2E
        tau_ref[pl.ds(p, 1), :] = tau2
        return carry

    lax.fori_loop(0, PEER_HEADS, head, 0)


def _peer_route(xnt, wpqt, keys, tn):
    d, n = xnt.shape
    rows = PEER_HEADS * PEER_NKEYS
    return pl.pallas_call(
        _peer_route_kernel,
        grid=(n // tn,),
        in_specs=[pl.BlockSpec((d, tn), lambda i: (0, i)), _const_spec((2 * rows, d)),
                  _const_spec((2 * PEER_HEADS, PEER_NKEYS, PEER_NKEYS))],
        out_specs=[pl.BlockSpec((rows, tn), lambda i: (0, i)), pl.BlockSpec((rows, tn), lambda i: (0, i)),
                   pl.BlockSpec((PEER_HEADS, tn), lambda i: (0, i))],
        out_shape=[jax.ShapeDtypeStruct((rows, n), F32), jax.ShapeDtypeStruct((rows, n), F32),
                   jax.ShapeDtypeStruct((PEER_HEADS, n), F32)],
        scratch_shapes=[pltpu.VMEM((2, PEER_TOPK, tn), F32)],
        compiler_params=_params(1),
        name="peer_route",
    )(xnt, wpqt, keys)


def _peer_dense_kernel(xnt_ref, u_ref, vt_ref, s1_ref, s2_ref, tau_ref, x1_ref, gfin_ref, out_ref,
                       acc_ref, act_ref, p_ref, *, te, tn, lc, final_norm):
    e = pl.program_id(1)

    @pl.when(e == 0)
    def _():
        acc_ref[...] = jnp.zeros_like(acc_ref)

    n_i1 = te // PEER_NKEYS

    n_chunks = tn // lc
    chunk = lambda k: slice(k * lc, (k + 1) * lc)

    def pre_activations(k):
        act_ref[:, chunk(k)] = _dot(u_ref[...], xnt_ref[:, chunk(k)])

    def accumulate(k):
        acc_ref[:, chunk(k)] += _dot(vt_ref[...], p_ref[:, chunk(k)])

    pre_activations(0)
    for k in range(n_chunks):
        cols = chunk(k)
        for il in range(n_i1):
            rows = slice(il * PEER_NKEYS, (il + 1) * PEER_NKEYS)
            i1 = e * n_i1 + il
            w = jnp.zeros((PEER_NKEYS, lc), F32)
            for p in range(PEER_HEADS):
                z = s2_ref[p * PEER_NKEYS:(p + 1) * PEER_NKEYS, cols] + s1_ref[pl.ds(p * PEER_NKEYS + i1, 1), cols]
                w = w + jnp.where(z >= tau_ref[p:p + 1, cols], jnp.exp2(z), 0.0)
            a = act_ref[rows, cols]
            gelu = 0.5 * a * (1.0 + lax.erf(a * (2.0 ** -0.5)))
            p_ref[rows, cols] = (w * gelu).astype(BF16)
            if il == 0 and k + 1 < n_chunks:
                pre_activations(k + 1)
            if il == min(1, n_i1 - 1) and k > 0:
                accumulate(k - 1)
    accumulate(n_chunks - 1)

    @pl.when(e == pl.num_programs(1) - 1)
    def _():
        y = x1_ref[...] + acc_ref[...].T
        if final_norm:
            y = _rms(y, gfin_ref[...])
        out_ref[...] = y


def _peer_dense(xnt, u, vt, s1, s2, tau, x1, gfin, tn, te, final_norm):
    d, n = xnt.shape
    rows = PEER_HEADS * PEER_NKEYS
    lc = min(tn, 2 * LANES)
    tokT = lambda r: pl.BlockSpec((r, tn), lambda i, e: (0, i))
    tok = pl.BlockSpec((tn, d), lambda i, e: (i, 0))
    return pl.pallas_call(
        functools.partial(_peer_dense_kernel, te=te, tn=tn, lc=lc, final_norm=final_norm),
        grid=(n // tn, PEER_N_EXPERTS // te),
        in_specs=[tokT(d), pl.BlockSpec((te, d), lambda i, e: (e, 0)), pl.BlockSpec((d, te), lambda i, e: (0, e)),
                  tokT(rows), tokT(rows), tokT(PEER_HEADS), tok, pl.BlockSpec((1, d), lambda i, e: (0, 0))],
        out_specs=tok,
        out_shape=jax.ShapeDtypeStruct((n, d), F32),
        scratch_shapes=[pltpu.VMEM((d, tn), F32), pltpu.VMEM((te, tn), F32), pltpu.VMEM((te, tn), BF16)],
        compiler_params=_params(2),
        name="peer_dense",
    )(xnt, u, vt, s1, s2, tau, x1, gfin)


_FOX_EXTRA = 6


def _fox_extra_base(head):
    return FOX_HD if head % 2 == 0 else 0


def _fox_selectors():
    width = FOX_HEADS * LANES
    sel_k = np.zeros((3, LANES, width), np.float32)
    sel_q = np.zeros((3, LANES, width), np.float32)
    const_k = np.zeros((1, width), np.float32)
    const_q = np.zeros((1, width), np.float32)
    for hd in range(FOX_HEADS):
        base = hd * LANES + _fox_extra_base(hd)
        for part in range(3):
            sel_k[part, hd, base + part] = 1.0
            const_q[0, base + part] = -1.0
            const_k[0, base + 3 + part] = 1.0
            sel_q[part, hd, base + 3 + part] = 1.0
    return jnp.asarray(sel_k, BF16), jnp.asarray(sel_q, BF16), jnp.asarray(const_k), jnp.asarray(const_q)


def _fox_proj_kernel(x_ref, g_ref, wq_ref, wk_ref, wv_ref, wf_ref, bf_ref, selk_ref, selq_ref, ck_ref, cq_ref,
                     q_ref, k_ref, v_ref, lf_ref, kaug_ref, qaugt_ref, vt_ref, carry_ref, *, tiles_per_seq):
    tm = x_ref.shape[0]

    @pl.when(pl.program_id(0) % tiles_per_seq == 0)
    def _():
        carry_ref[...] = jnp.zeros_like(carry_ref)

    h = _rms(x_ref[...], g_ref[...]).astype(BF16)
    q = _dot(h, wq_ref[...]) * (FOX_HD ** -0.5)
    q_ref[...] = q
    k = _dot(h, wk_ref[...])
    k_ref[...] = k
    v = _dot(h, wv_ref[...])
    v_ref[...] = v
    vt_ref[...] = v.T.astype(BF16)
    lane = lax.broadcasted_iota(jnp.int32, (tm, LANES), 1)
    lf = jnp.where(lane < FOX_HEADS, _log_sigmoid(_dot(h, wf_ref[...]) + bf_ref[...]), 0.0)
    lf_ref[...] = lf[:, :FOX_HEADS]
    rows = lax.broadcasted_iota(jnp.int32, (tm, tm), 0)
    cols = lax.broadcasted_iota(jnp.int32, (tm, tm), 1)
    c = _dot((rows >= cols).astype(F32), lf, precision=HIGHEST) + carry_ref[...]
    carry_ref[...] = c[tm - 1:tm, :]
    c2 = c * LOG2E
    hi = c2.astype(BF16)
    r1 = c2 - hi.astype(F32)
    mid = r1.astype(BF16)
    lo = (r1 - mid.astype(F32)).astype(BF16)
    ext_k = _dot(hi, selk_ref[0]) + _dot(mid, selk_ref[1]) + _dot(lo, selk_ref[2]) + ck_ref[...]
    ext_q = _dot(hi, selq_ref[0]) + _dot(mid, selq_ref[1]) + _dot(lo, selq_ref[2]) + cq_ref[...]
    for hd in range(FOX_HEADS):
        src = slice((hd // 2) * LANES, (hd // 2 + 1) * LANES)
        dst = slice(hd * LANES, (hd + 1) * LANES)
        own = (lane < FOX_HD) if hd % 2 == 0 else (lane >= FOX_HD)
        kaug_ref[:, dst] = jnp.where(own, k[:, src], ext_k[:, dst]).astype(BF16)
        qaugt_ref[dst, :] = jnp.where(own, q[:, src] * LOG2E, ext_q[:, dst]).T.astype(BF16)


def _fox_proj(x, g, wq, wk, wv, wf, bf, tm, seq):
    n, d = x.shape
    width = FOX_HEADS * LANES
    tok = pl.BlockSpec((tm, d), lambda i: (i, 0))
    sel_k, sel_q, const_k, const_q = _fox_selectors()
    return pl.pallas_call(
        functools.partial(_fox_proj_kernel, tiles_per_seq=max(seq // tm, 1)),
        grid=(n // tm,),
        in_specs=[tok, _const_spec((1, d)), _const_spec((d, d)), _const_spec((d, d)), _const_spec((d, d)),
                  _const_spec((d, LANES)), _const_spec((1, LANES)), _const_spec((3, LANES, width)),
                  _const_spec((3, LANES, width)), _const_spec((1, width)), _const_spec((1, width))],
        out_specs=[tok, tok, tok, pl.BlockSpec((tm, FOX_HEADS), lambda i: (i, 0)),
                   pl.BlockSpec((tm, width), lambda i: (i, 0)), pl.BlockSpec((width, tm), lambda i: (0, i)),
                   pl.BlockSpec((d, tm), lambda i: (0, i))],
        out_shape=[jax.ShapeDtypeStruct((n, d), F32), jax.ShapeDtypeStruct((n, d), F32),
                   jax.ShapeDtypeStruct((n, d), F32), jax.ShapeDtypeStruct((n, FOX_HEADS), F32),
                   jax.ShapeDtypeStruct((n, width), BF16), jax.ShapeDtypeStruct((width, n), BF16),
                   jax.ShapeDtypeStruct((d, n), BF16)],
        scratch_shapes=[pltpu.VMEM((1, LANES), F32)],
        compiler_params=_params(1),
        name="fox_proj",
    )(x, g, wq, wk, wv, wf, bf, sel_k, sel_q, const_k, const_q)


def _fox_flash_kernel(qi_ref, ki_ref, qt_ref, k_ref, vt_ref, o_ref, s_ref, m_ref, l_ref, acc_ref, *, tq, tk):
    t = pl.program_id(2)
    qi = qi_ref[t]
    ki = ki_ref[t]

    @pl.when(ki == 0)
    def _():
        m_ref[...] = jnp.full_like(m_ref, -jnp.inf)
        l_ref[...] = jnp.zeros_like(l_ref)
        acc_ref[...] = jnp.zeros_like(acc_ref)

    def step(masked):
        if masked:
            keep = (lax.broadcasted_iota(jnp.int32, (tk, tq), 0) <= lax.broadcasted_iota(jnp.int32, (tk, tq), 1))
        for hh in range(2):
            feat = slice(hh * LANES, (hh + 1) * LANES)
            s = _dot(k_ref[:, feat], qt_ref[feat, :])
            if masked:
                s = jnp.where(keep, s, -jnp.inf)
            s_ref[hh] = s
            m_prev = m_ref[hh]
            m_new = jnp.maximum(m_prev, jnp.max(s, axis=0, keepdims=True))
            alpha = jnp.exp2(m_prev - m_new)
            p = jnp.exp2(s_ref[hh] - m_new)
            l_ref[hh] = alpha * l_ref[hh] + jnp.sum(p, axis=0, keepdims=True)
            m_ref[hh] = m_new
            acc_ref[hh] = acc_ref[hh] * alpha + _dot(vt_ref[hh * FOX_HD:(hh + 1) * FOX_HD, :], p.astype(BF16))

    @pl.when(ki < qi)
    def _():
        step(False)

    @pl.when(ki == qi)
    def _():
        step(True)
        for hh in range(2):
            o_ref[hh * FOX_HD:(hh + 1) * FOX_HD, :] = acc_ref[hh] / l_ref[hh]


def _fox_flash(qaugt, kaug, vt, batch, tq):
    d, n = vt.shape
    s = n // batch
    nq = s // tq
    tri = [(qi, ki) for qi in range(nq) for ki in range(qi + 1)]
    qi_of = jnp.array([a for a, _ in tri], jnp.int32)
    ki_of = jnp.array([b for _, b in tri], jnp.int32)
    grid_spec = pltpu.PrefetchScalarGridSpec(
        num_scalar_prefetch=2,
        grid=(batch, FOX_PAIRS, len(tri)),
        in_specs=[pl.BlockSpec((2 * LANES, tq), lambda b, pr, t, qi, ki: (pr, b * nq + qi[t])),
                  pl.BlockSpec((tq, 2 * LANES), lambda b, pr, t, qi, ki: (b * nq + ki[t], pr)),
                  pl.BlockSpec((LANES, tq), lambda b, pr, t, qi, ki: (pr, b * nq + ki[t]))],
        out_specs=pl.BlockSpec((LANES, tq), lambda b, pr, t, qi, ki: (pr, b * nq + qi[t])),
        scratch_shapes=[pltpu.VMEM((2, tq, tq), F32), pltpu.VMEM((2, 1, tq), F32), pltpu.VMEM((2, 1, tq), F32),
                        pltpu.VMEM((2, FOX_HD, tq), F32)],
    )
    return pl.pallas_call(
        functools.partial(_fox_flash_kernel, tq=tq, tk=tq),
        grid_spec=grid_spec,
        out_shape=jax.ShapeDtypeStruct((d, n), F32),
        compiler_params=_params(3),
        name="fox_flash",
    )(qi_of, ki_of, qaugt, kaug, vt)


def _fox_sample_kernel(pt_ref, q_ref, lfc_ref, kn_ref, vn_ref, lfn_ref, *rest, pages_per_step, t_new):
    k_refs = rest[:pages_per_step]
    v_refs = rest[pages_per_step:2 * pages_per_step]
    lf_refs = rest[2 * pages_per_step:3 * pages_per_step]
    o_ref, cnb_ref, m_ref, l_ref, acc_ref, tot_ref = rest[3 * pages_per_step:]
    step = pl.program_id(1)
    t_pad = q_ref.shape[2]
    page = lf_refs[0].shape[-1]
    rows_i = lax.broadcasted_iota(jnp.int32, (page, page), 0)
    cols_i = lax.broadcasted_iota(jnp.int32, (page, page), 1)

    q = q_ref[0].astype(BF16)

    def per_head_rows(a):
        return jnp.stack([jnp.broadcast_to(a[h:h + 1, :], (t_pad, page)) for h in range(FOX_HEADS)], axis=0)

    def attend(kt, vt, bias):
        s = lax.dot_general(q, kt.astype(BF16), (((2,), (1,)), ((0,), (0,))), preferred_element_type=F32) + bias
        m_prev = m_ref[...]
        m_new = jnp.maximum(m_prev, jnp.max(s, axis=2, keepdims=True))
        alpha = jnp.exp(m_prev - m_new)
        p = jnp.exp(s - m_new)
        l_ref[...] = alpha * l_ref[...] + jnp.sum(p, axis=2, keepdims=True)
        m_ref[...] = m_new
        pv = lax.dot_general(p.astype(BF16), vt.astype(BF16), (((2,), (2,)), ((0,), (0,))), preferred_element_type=F32)
        acc_ref[...] = acc_ref[...] * alpha[:, :, :FOX_HD] + pv

    @pl.when(step == 0)
    def _():
        m_ref[...] = jnp.full_like(m_ref, -jnp.inf)
        l_ref[...] = jnp.zeros_like(l_ref)
        acc_ref[...] = jnp.zeros_like(acc_ref)
        tot_ref[...] = jnp.zeros_like(tot_ref)
        col = lfc_ref[0]
        row_t = lax.broadcasted_iota(jnp.int32, (1, t_pad, 1), 1)
        run = jnp.zeros((FOX_HEADS, 1, 1), F32)
        cn = jnp.zeros((FOX_HEADS, t_pad, 1), F32)
        for t in range(t_new):
            run = run + col[:, t:t + 1, :]
            cn = jnp.where(row_t == t, run, cn)
        cnb_ref[...] = jnp.broadcast_to(cn, cnb_ref.shape)
        cnt = _dot(lfn_ref[0], (rows_i <= cols_i).astype(F32), precision=HIGHEST)
        causal = (lax.broadcasted_iota(jnp.int32, (1, t_pad, page), 2)
                  <= lax.broadcasted_iota(jnp.int32, (1, t_pad, page), 1))
        attend(kn_ref[0], vn_ref[0], jnp.where(causal, cnb_ref[...] - per_head_rows(cnt), -jnp.inf))

    strict = (rows_i > cols_i).astype(F32)
    ones = jnp.ones((page, page), F32)
    for i in range(pages_per_step):
        lft = lf_refs[i][0]
        r = _dot(lft, strict, precision=HIGHEST) + tot_ref[...]
        tot_ref[...] = tot_ref[...] + _dot(lft, ones, precision=HIGHEST)
        attend(k_refs[i][0, 0], v_refs[i][0, 0], cnb_ref[...] + per_head_rows(r))

    @pl.when(step == pl.num_programs(1) - 1)
    def _():
        o_ref[0] = acc_ref[...] / l_ref[...][:, :, :FOX_HD]


def _fox_sample(q, lf_col, k_new, v_new, lft_new, cache_k, cache_v, cache_lft, page_table, layer, pages_per_step):
    db, _, t_pad, _ = q.shape
    t_new = lf_col.shape[2]
    n_pages = page_table.shape[1]
    page = cache_k.shape[-1]
    assert page == LANES and cache_k.shape[2:4] == (FOX_HEADS, FOX_HD)
    n_steps = n_pages // pages_per_step
    lf_col = jnp.pad(lf_col, ((0, 0), (0, 0), (0, t_pad - t_new), (0, 0)))

    def page_of(i):
        return lambda b, s, pt: pt[b, n_pages - 1 - (s * pages_per_step + i)]

    def kv_map(i):
        idx = page_of(i)
        return lambda b, s, pt: (layer, idx(b, s, pt), 0, 0, 0)

    def lf_map(i):
        idx = page_of(i)
        return lambda b, s, pt: (idx(b, s, pt), 0, 0)

    per_b = lambda shape: pl.BlockSpec((1,) + shape, lambda b, s, pt: (b,) + (0,) * len(shape))
    kv_blk = (FOX_HEADS, FOX_HD, page)
    in_specs = [per_b((FOX_HEADS, t_pad, FOX_HD)), per_b((FOX_HEADS, t_pad, 1)), per_b(kv_blk), per_b(kv_blk),
                per_b((FOX_HEADS, page))]
    in_specs += [pl.BlockSpec((1, 1) + kv_blk, kv_map(i)) for i in range(pages_per_step)]
    in_specs += [pl.BlockSpec((1, 1) + kv_blk, kv_map(i)) for i in range(pages_per_step)]
    in_specs += [pl.BlockSpec((1, FOX_HEADS, page), lf_map(i)) for i in range(pages_per_step)]
    grid_spec = pltpu.PrefetchScalarGridSpec(
        num_scalar_prefetch=1,
        grid=(db, n_steps),
        in_specs=in_specs,
        out_specs=per_b((FOX_HEADS, t_pad, FOX_HD)),
        scratch_shapes=[pltpu.VMEM((FOX_HEADS, t_pad, LANES), F32), pltpu.VMEM((FOX_HEADS, t_pad, LANES), F32),
                        pltpu.VMEM((FOX_HEADS, t_pad, LANES), F32), pltpu.VMEM((FOX_HEADS, t_pad, FOX_HD), F32),
                        pltpu.VMEM((FOX_HEADS, page), F32)],
    )
    return pl.pallas_call(
        functools.partial(_fox_sample_kernel, pages_per_step=pages_per_step, t_new=t_new),
        grid_spec=grid_spec,
        out_shape=jax.ShapeDtypeStruct((db, FOX_HEADS, t_pad, FOX_HD), F32),
        compiler_params=_params(2),
        name="fox_sample",
    )(page_table, q, lf_col, k_new, v_new, lft_new,
      *([cache_k] * pages_per_step), *([cache_v] * pages_per_step), *([cache_lft] * pages_per_step))


def _tile(n, pref):
    return pref if n % pref == 0 else n


def _peer(x1, xnt, wpqt, keys, u, vt, gfin, final_norm):
    n = x1.shape[0]
    s1, s2, tau = _peer_route(xnt, wpqt, keys, _tile(n, 256))
    return _peer_dense(xnt, u, vt, s1, s2, tau, x1, gfin, _tile(n, 1024), 512, final_norm)


def kernel(x_prompt, x_sample, state_gla, cache_k, cache_v, cache_logf, page_table, g_mix, g_ffn, g_final, gla_w_q, gla_w_k, gla_w_v, gla_w_g1, gla_w_g2, gla_b_g, gla_w_r, gla_b_r, gla_g_o, gla_w_o, fox_w_q, fox_w_k, fox_w_v, fox_w_f, fox_b_f, fox_w_o, peer_w_q, peer_keys, peer_u, peer_v):
    batch, seq, d = x_prompt.shape
    db, t_new, _ = x_sample.shape
    n_p, n_s = batch * seq, db * t_new
    xp = x_prompt.reshape(n_p, d)
    xs = x_sample.reshape(n_s, d)
    row = lambda a: a.reshape(1, -1).astype(F32)
    gfin = row(g_final)

    peer = []
    for i in range(peer_w_q.shape[0]):
        peer.append((peer_w_q[i].T.astype(BF16),
                     peer_keys[i].reshape(2 * PEER_HEADS, PEER_NKEYS, PEER_NKEYS).astype(BF16),
                     peer_u[i].astype(BF16), peer_v[i].T.astype(BF16)))

    rank = gla_w_g1.shape[-1]
    wg1 = jnp.pad(gla_w_g1[0], ((0, 0), (0, LANES - rank))).astype(BF16)
    wg2 = jnp.pad(gla_w_g2[0], ((0, LANES - rank), (0, 0))).astype(BF16)
    gla_w = (row(g_mix[0]), gla_w_q[0].astype(BF16), gla_w_k[0].astype(BF16), gla_w_v[0].astype(BF16),
             gla_w_r[0].astype(BF16), wg1, wg2, row(gla_b_g[0]), row(gla_b_r[0]))
    wo0 = gla_w_o[0].astype(BF16)

    def gla_layer(x, b, s, s0t, chunk, sub, pad_to):
        n = x.shape[0]
        q, k, v, lg, r = _gla_proj(x, *gla_w, _tile(n, 512))
        shp = lambda a: a.reshape(b, s, a.shape[-1])
        q, k, v, lg = shp(q), shp(k), shp(v), shp(lg)
        if pad_to > s:
            padseq = lambda a: jnp.pad(a, ((0, 0), (0, pad_to - s), (0, 0)))
            q, k, v, lg = padseq(q), padseq(k), padseq(v), padseq(lg)
        o, sfin_t = _gla_scan(q, k, v, lg, s0t, chunk, sub)
        o = o[:, :s].reshape(n, -1)
        x1, xnt = _post_mix(o, r, row(gla_g_o[0]), wo0, x, row(g_ffn[0]), _tile(n, 512))
        return _peer(x1, xnt, *peer[0], gfin, False), jnp.swapaxes(sfin_t, -1, -2)

    s0p = jnp.zeros((batch, GLA_HEADS, GLA_DV, GLA_DK), F32)
    xp, gla_sp = gla_layer(xp, batch, seq, s0p, GLA_CHUNK, GLA_SUB, seq)
    s0s = jnp.swapaxes(state_gla[0], -1, -2).astype(F32)
    xs, gla_ss = gla_layer(xs, db, t_new, s0s, 8, 8, 8)

    wf = jnp.pad(fox_w_f[0], ((0, 0), (0, LANES - FOX_HEADS))).astype(BF16)
    bf = jnp.pad(fox_b_f[0], (0, LANES - FOX_HEADS)).reshape(1, LANES).astype(F32)
    fox_w = (row(g_mix[1]), fox_w_q[0].astype(BF16), fox_w_k[0].astype(BF16), fox_w_v[0].astype(BF16), wf, bf)
    wo1 = fox_w_o[0].astype(BF16)

    _, kp, vp, lfp, kaug, qaugt, vpt = _fox_proj(xp, *fox_w, _tile(n_p, 512), seq)
    opt = _fox_flash(qaugt, kaug, vpt, batch, _tile(seq, 1024))
    x1, xnt = _post_mix(opt, None, None, wo1, xp, row(g_ffn[1]), _tile(n_p, 512))
    yp = _peer(x1, xnt, *peer[1], gfin, True)

    qs, ks, vs, lfs, _, _, _ = _fox_proj(xs, *fox_w, _tile(n_s, 512), t_new)
    page = cache_k.shape[2]
    t_pad = 8
    heads_major = lambda a: a.reshape(db, t_new, FOX_HEADS, -1).transpose(0, 2, 1, 3)
    q_h = jnp.pad(heads_major(qs), ((0, 0), (0, 0), (0, t_pad - t_new), (0, 0)))
    lf_h = heads_major(lfs)
    keys_last = lambda a: jnp.pad(heads_major(a).transpose(0, 1, 3, 2), ((0, 0), (0, 0), (0, 0), (0, page - t_new)))
    k_new, v_new = keys_last(ks), keys_last(vs)
    lft_new = jnp.pad(lf_h[..., 0], ((0, 0), (0, 0), (0, page - t_new)))
    ckt = cache_k.transpose(0, 1, 3, 4, 2)
    cvt = cache_v.transpose(0, 1, 3, 4, 2)
    clft = jnp.swapaxes(cache_logf[0], -1, -2)
    os_h = _fox_sample(q_h, lf_h, k_new, v_new, lft_new, ckt, cvt, clft, page_table, 0, 4)
    ost = os_h[:, :, :t_new, :].transpose(1, 3, 0, 2).reshape(d, n_s)
    x1, xnt = _post_mix(ost, None, None, wo1, xs, row(g_ffn[1]), _tile(n_s, 512))
    ys = _peer(x1, xnt, *peer[1], gfin, True)

    return (yp.reshape(batch, seq, d), ys.reshape(db, t_new, d), gla_sp[None], gla_ss[None],
            kp.reshape(1, batch, seq, FOX_HEADS, FOX_HD), vp.reshape(1, batch, seq, FOX_HEADS, FOX_HD),
            lfp.reshape(1, batch, seq, FOX_HEADS),
            ks.reshape(1, db, t_new, FOX_HEADS, FOX_HD), vs.reshape(1, db, t_new, FOX_HEADS, FOX_HD),
            lfs.reshape(1, db, t_new, FOX_HEADS))
```

```python
import functools

import numpy as np
import jax
import jax.numpy as jnp
from jax import lax
from jax.experimental import pallas as pl
from jax.experimental.pallas import tpu as pltpu

F32 = jnp.float32
BF16 = jnp.bfloat16
HIGHEST = lax.Precision.HIGHEST

D_MODEL = 1024
GLA_HEADS = 4
GLA_DK = 128
GLA_DV = 256
GLA_TAU = 16.0
GLA_CHUNK = 64
GLA_SUB = 16
FOX_HEADS = 16
FOX_HD = 64
FOX_PAIRS = FOX_HEADS * FOX_HD // 128
PEER_HEADS = 8
PEER_NKEYS = 128
PEER_TOPK = 16
PEER_N_EXPERTS = PEER_NKEYS * PEER_NKEYS
RMS_EPS = 1e-6
LOG2E = 1.4426950408889634
LANES = 128
VMEM_LIMIT = 56 * 1024 * 1024


def _params(n_axes, vmem=VMEM_LIMIT):
    return pltpu.CompilerParams(dimension_semantics=("arbitrary",) * n_axes, vmem_limit_bytes=vmem)


def _rms(x, g):
    return x * lax.rsqrt(jnp.mean(x * x, axis=-1, keepdims=True) + RMS_EPS) * g


def _dot(a, b, precision=None):
    return jnp.dot(a, b, preferred_element_type=F32, precision=precision)


def _dot_nt(a, b):
    return lax.dot_general(a, b, (((1,), (1,)), ((), ())), preferred_element_type=F32)


def _dot_tn(a, b):
    return lax.dot_general(a, b, (((0,), (0,)), ((), ())), preferred_element_type=F32)


def _log_sigmoid(z):
    return jnp.minimum(z, 0.0) - jnp.log1p(jnp.exp(-jnp.abs(z)))


def _const_spec(shape):
    return pl.BlockSpec(shape, lambda *_: (0,) * len(shape))


def _gla_proj_kernel(x_ref, g_ref, wq_ref, wk_ref, wv_ref, wr_ref, wg1_ref, wg2_ref, bg_ref, br_ref,
                     q_ref, k_ref, v_ref, lg_ref, r_ref):
    h = _rms(x_ref[...], g_ref[...]).astype(BF16)
    q_ref[...] = _dot(h, wq_ref[...]) * (GLA_DK ** -0.5)
    k_ref[...] = _dot(h, wk_ref[...])
    v_ref[...] = _dot(h, wv_ref[...])
    u = _dot(h, wr_ref[...]) + br_ref[...]
    r_ref[...] = u / (1.0 + jnp.exp(-u))
    t = _dot(h, wg1_ref[...]).astype(BF16)
    z = _dot(t, wg2_ref[...]) + bg_ref[...]
    lg_ref[...] = _log_sigmoid(z) * (1.0 / GLA_TAU)


def _gla_proj(x, g, wq, wk, wv, wr, wg1, wg2, bg, br, tm):
    n, d = x.shape
    gk = GLA_HEADS * GLA_DK
    gv = GLA_HEADS * GLA_DV
    tok = lambda w: pl.BlockSpec((tm, w), lambda i: (i, 0))
    return pl.pallas_call(
        _gla_proj_kernel,
        grid=(n // tm,),
        in_specs=[tok(d), _const_spec((1, d)), _const_spec((d, gk)), _const_spec((d, gk)), _const_spec((d, gv)),
                  _const_spec((d, gv)), _const_spec((d, LANES)), _const_spec((LANES, gk)), _const_spec((1, gk)),
                  _const_spec((1, gv))],
        out_specs=[tok(gk), tok(gk), tok(gv), tok(gk), tok(gv)],
        out_shape=[jax.ShapeDtypeStruct((n, gk), F32), jax.ShapeDtypeStruct((n, gk), F32),
                   jax.ShapeDtypeStruct((n, gv), F32), jax.ShapeDtypeStruct((n, gk), F32),
                   jax.ShapeDtypeStruct((n, gv), F32)],
        compiler_params=_params(1),
        name="gla_proj",
    )(x, g, wq, wk, wv, wr, wg1, wg2, bg, br)


def _gla_scan_kernel(q_ref, k_ref, v_ref, g_ref, s0_ref, o_ref, sfin_ref, st_ref, *, chunk, sub):
    j = pl.program_id(1)

    @pl.when(j == 0)
    def _():
        st_ref[...] = s0_ref[0]

    rows = lax.broadcasted_iota(jnp.int32, (chunk, chunk), 0)
    cols = lax.broadcasted_iota(jnp.int32, (chunk, chunk), 1)
    tril = (rows >= cols).astype(F32)
    t_idx = lax.broadcasted_iota(jnp.int32, (sub, sub, 1), 0)
    s_idx = lax.broadcasted_iota(jnp.int32, (sub, sub, 1), 1)
    causal = t_idx >= s_idx
    ones = jnp.ones((LANES, LANES), BF16)
    n_sub = chunk // sub

    for h in range(GLA_HEADS):
        ks = slice(h * GLA_DK, (h + 1) * GLA_DK)
        vs = slice(h * GLA_DV, (h + 1) * GLA_DV)
        qh = q_ref[0, :, ks]
        kh = k_ref[0, :, ks]
        vh = v_ref[0, :, vs]
        b = _dot(tril, g_ref[0, :, ks], precision=HIGHEST)
        b_last = b[chunk - 1:chunk, :]
        st = st_ref[h]
        o = _dot_nt((qh * jnp.exp(b)).astype(BF16), st.astype(BF16))
        parts = []
        for i in range(n_sub):
            bt = b[i * sub:(i + 1) * sub]
            qi = qh[i * sub:(i + 1) * sub]
            acc0 = jnp.zeros((sub, LANES), F32)
            acc1 = jnp.zeros((sub, LANES), F32)
            for jj in range(i + 1):
                bs = b[jj * sub:(jj + 1) * sub]
                kj = kh[jj * sub:(jj + 1) * sub]
                vj = vh[jj * sub:(jj + 1) * sub]
                diff = bt[:, None, :] - bs[None, :, :]
                if jj == i:
                    diff = jnp.where(causal, diff, -jnp.inf)
                a = (qi[:, None, :] * kj[None, :, :]) * jnp.exp(diff)
                att = _dot(a.reshape(sub * sub, GLA_DK).astype(BF16), ones).reshape(sub, sub, LANES)
                acc0 = acc0 + jnp.sum(att * vj[None, :, :LANES], axis=1)
                acc1 = acc1 + jnp.sum(att * vj[None, :, LANES:], axis=1)
            parts.append(jnp.concatenate([acc0, acc1], axis=1))
        o_ref[0, :, vs] = o + jnp.concatenate(parts, axis=0)
        kd = (kh * jnp.exp(b_last - b)).astype(BF16)
        st_ref[h] = st * jnp.exp(b_last) + _dot_tn(vh.astype(BF16), kd)

    @pl.when(j == pl.num_programs(1) - 1)
    def _():
        sfin_ref[0] = st_ref[...]


def _gla_scan(q, k, v, lg, s0t, chunk, sub):
    b, s, gk = q.shape
    gv = v.shape[-1]
    seq = lambda w: pl.BlockSpec((1, chunk, w), lambda bi, j: (bi, j, 0))
    st_spec = pl.BlockSpec((1, GLA_HEADS, GLA_DV, GLA_DK), lambda bi, j: (bi, 0, 0, 0))
    return pl.pallas_call(
        functools.partial(_gla_scan_kernel, chunk=chunk, sub=sub),
        grid=(b, s // chunk),
        in_specs=[seq(gk), seq(gk), seq(gv), seq(gk), st_spec],
        out_specs=[seq(gv), st_spec],
        out_shape=[jax.ShapeDtypeStruct((b, s, gv), F32),
                   jax.ShapeDtypeStruct((b, GLA_HEADS, GLA_DV, GLA_DK), F32)],
        scratch_shapes=[pltpu.VMEM((GLA_HEADS, GLA_DV, GLA_DK), F32)],
        compiler_params=_params(2),
        name="gla_scan",
    )(q, k, v, lg, s0t)


def _post_mix_kernel(*refs, gla):
    if gla:
        o_ref, r_ref, go_ref, wo_ref, x_ref, gf_ref, x1_ref, xnt_ref = refs
        parts = []
        for h in range(GLA_HEADS):
            vs = slice(h * GLA_DV, (h + 1) * GLA_DV)
            parts.append(_rms(o_ref[:, vs], go_ref[...]) * r_ref[:, vs])
        a = jnp.concatenate(parts, axis=1)
    else:
        o_ref, wo_ref, x_ref, gf_ref, x1_ref, xnt_ref = refs
        a = o_ref[...].T
    x1 = x_ref[...] + _dot(a.astype(BF16), wo_ref[...])
    x1_ref[...] = x1
    xnt_ref[...] = _rms(x1, gf_ref[...]).T.astype(BF16)


def _post_mix(o, r, go, wo, x, gf, tm):
    n, d = x.shape
    tok = pl.BlockSpec((tm, d), lambda i: (i, 0))
    gla = r is not None
    if gla:
        args = (o, r, go, wo, x, gf)
        in_specs = [tok, tok, _const_spec((1, GLA_DV)), _const_spec((d, d)), tok, _const_spec((1, d))]
    else:
        args = (o, wo, x, gf)
        in_specs = [pl.BlockSpec((d, tm), lambda i: (0, i)), _const_spec((d, d)), tok, _const_spec((1, d))]
    return pl.pallas_call(
        functools.partial(_post_mix_kernel, gla=gla),
        grid=(n // tm,),
        in_specs=in_specs,
        out_specs=[tok, pl.BlockSpec((d, tm), lambda i: (0, i))],
        out_shape=[jax.ShapeDtypeStruct((n, d), F32), jax.ShapeDtypeStruct((d, n), BF16)],
        compiler_params=_params(1),
        name="post_mix_gla" if gla else "post_mix_fox",
    )(*args)


_PEER_CAND_ROWS = 80


def _peer_candidates(v1, v2):
    r8 = lax.broadcasted_iota(jnp.int32, (8, 1), 0)
    pieces = []
    for a in range(8):
        z = v1[a:a + 1] + v2[0:8]
        pieces.append(z if a == 0 else jnp.where(r8 < PEER_TOPK // (a + 1), z, -jnp.inf))
    pieces.append(v1[0:1] + v2[8:16])
    pieces.append(v1[8:16] + v2[0:1])
    return jnp.concatenate(pieces, axis=0)


def _peer_route_kernel(xnt_ref, wpqt_ref, keys_ref, s1_ref, s2_ref, tau_ref, top_ref):
    xnt = xnt_ref[...]

    def head(p, carry):
        s = []
        for half in range(2):
            row = pl.multiple_of((2 * p + half) * PEER_NKEYS, PEER_NKEYS)
            qt = _dot(wpqt_ref[pl.ds(row, PEER_NKEYS), :], xnt).astype(BF16)
            sc = _dot(keys_ref[2 * p + half], qt)
            s.append(sc)
            cur = sc
            for rnk in range(PEER_TOPK):
                mx = jnp.max(cur, axis=0, keepdims=True)
                top_ref[half, rnk:rnk + 1, :] = mx
                cur = jnp.where(cur == mx, -jnp.inf, cur)
        v1 = top_ref[0]
        v2 = top_ref[1]
        m = v1[0:1] + v2[0:1]
        v1m = v1 - m
        z = _peer_candidates(v1m, v2)
        cur = z
        for rnk in range(PEER_TOPK):
            tau = jnp.max(cur, axis=0, keepdims=True)
            if rnk + 1 < PEER_TOPK:
                cur = jnp.where(cur == tau, -jnp.inf, cur)
        sel = z >= tau
        log_z = jnp.log(jnp.sum(jnp.where(sel, jnp.exp(z), 0.0), axis=0, keepdims=True))
        z2 = _peer_candidates((v1m - log_z) * LOG2E, v2 * LOG2E)
        tau2 = jnp.min(jnp.where(sel, z2, jnp.inf), axis=0, keepdims=True)
        row = pl.multiple_of(p * PEER_NKEYS, PEER_NKEYS)
        s1_ref[pl.ds(row, PEER_NKEYS), :] = ((s[0] - m) - log_z) * LOG2E
        s2_ref[pl.ds(row, PEER_NKEYS), :] = s[1] * LOG2E
        tau_ref[pl.ds(p, 1), :] = tau2
        return carry

    lax.fori_loop(0, PEER_HEADS, head, 0)


def _peer_route(xnt, wpqt, keys, tn):
    d, n = xnt.shape
    rows = PEER_HEADS * PEER_NKEYS
    return pl.pallas_call(
        _peer_route_kernel,
        grid=(n // tn,),
        in_specs=[pl.BlockSpec((d, tn), lambda i: (0, i)), _const_spec((2 * rows, d)),
                  _const_spec((2 * PEER_HEADS, PEER_NKEYS, PEER_NKEYS))],
        out_specs=[pl.BlockSpec((rows, tn), lambda i: (0, i)), pl.BlockSpec((rows, tn), lambda i: (0, i)),
                   pl.BlockSpec((PEER_HEADS, tn), lambda i: (0, i))],
        out_shape=[jax.ShapeDtypeStruct((rows, n), F32), jax.ShapeDtypeStruct((rows, n), F32),
                   jax.ShapeDtypeStruct((PEER_HEADS, n), F32)],
        scratch_shapes=[pltpu.VMEM((2, PEER_TOPK, tn), F32)],
        compiler_params=_params(1),
        name="peer_route",
    )(xnt, wpqt, keys)


def _peer_dense_kernel(xnt_ref, u_ref, unext_ref, vt_ref, vtprev_ref, s1_ref, s2_ref, tau_ref, x1_ref, gfin_ref,
                       out_ref, acc_ref, act_ref, act0_ref, p_ref, plast_ref, *, te, tn, lc, final_norm):
    e = pl.program_id(1)
    n_e = pl.num_programs(1)
    slot = e % 2
    other = 1 - slot
    n_i1 = te // PEER_NKEYS
    n_chunks = tn // lc
    last = n_chunks - 1
    chunk = lambda k: slice(k * lc, (k + 1) * lc)

    @pl.when(e == 0)
    def _():
        acc_ref[...] = jnp.zeros_like(acc_ref)
        act0_ref[slot] = _dot(u_ref[...], xnt_ref[:, chunk(0)])
        plast_ref[other] = jnp.zeros((te, lc), BF16)

    for k in range(n_chunks):
        cols = chunk(k)
        for il in range(n_i1):
            rows = slice(il * PEER_NKEYS, (il + 1) * PEER_NKEYS)
            i1 = e * n_i1 + il
            w = None
            for p in range(PEER_HEADS):
                z = s2_ref[p * PEER_NKEYS:(p + 1) * PEER_NKEYS, cols] + s1_ref[pl.ds(p * PEER_NKEYS + i1, 1), cols]
                term = jnp.where(z >= tau_ref[p:p + 1, cols], jnp.exp2(z), 0.0)
                w = term if w is None else w + term
            a = act0_ref[slot, rows, :] if k == 0 else act_ref[rows, cols]
            gelu = 0.5 * a * (1.0 + lax.erf(a * (2.0 ** -0.5)))
            pv = (w * gelu).astype(BF16)
            if k == last:
                plast_ref[slot, rows, :] = pv
            else:
                p_ref[rows, cols] = pv
            if il == 0:
                if k < last:
                    act_ref[:, chunk(k + 1)] = _dot(u_ref[...], xnt_ref[:, chunk(k + 1)])
                else:
                    act0_ref[other] = _dot(unext_ref[...], xnt_ref[:, chunk(0)])
            if il == min(1, n_i1 - 1):
                if k > 0:
                    acc_ref[:, chunk(k - 1)] += _dot(vt_ref[...], p_ref[:, chunk(k - 1)])
                else:
                    acc_ref[:, chunk(last)] += _dot(vtprev_ref[...], plast_ref[other])

    @pl.when(e == n_e - 1)
    def _():
        acc_ref[:, chunk(last)] += _dot(vt_ref[...], plast_ref[slot])
        y = x1_ref[...] + acc_ref[...].T
        if final_norm:
            y = _rms(y, gfin_ref[...])
        out_ref[...] = y


def _peer_dense(xnt, u, vt, s1, s2, tau, x1, gfin, tn, te, final_norm):
    d, n = xnt.shape
    rows = PEER_HEADS * PEER_NKEYS
    lc = min(tn, 2 * LANES)
    n_e = PEER_N_EXPERTS // te
    tokT = lambda r: pl.BlockSpec((r, tn), lambda i, e: (0, i))
    tok = pl.BlockSpec((tn, d), lambda i, e: (i, 0))
    return pl.pallas_call(
        functools.partial(_peer_dense_kernel, te=te, tn=tn, lc=lc, final_norm=final_norm),
        grid=(n // tn, n_e),
        in_specs=[tokT(d),
                  pl.BlockSpec((te, d), lambda i, e: (e, 0)),
                  pl.BlockSpec((te, d), lambda i, e: (jnp.minimum(e + 1, n_e - 1), 0)),
                  pl.BlockSpec((d, te), lambda i, e: (0, e)),
                  pl.BlockSpec((d, te), lambda i, e: (0, jnp.maximum(e - 1, 0))),
                  tokT(rows), tokT(rows), tokT(PEER_HEADS), tok, pl.BlockSpec((1, d), lambda i, e: (0, 0))],
        out_specs=tok,
        out_shape=jax.ShapeDtypeStruct((n, d), F32),
        scratch_shapes=[pltpu.VMEM((d, tn), F32), pltpu.VMEM((te, tn), F32), pltpu.VMEM((2, te, lc), F32),
                        pltpu.VMEM((te, tn), BF16), pltpu.VMEM((2, te, lc), BF16)],
        compiler_params=_params(2),
        name="peer_dense",
    )(xnt, u, u, vt, vt, s1, s2, tau, x1, gfin)


_FOX_EXTRA = 6


def _fox_extra_base(head):
    return FOX_HD if head % 2 == 0 else 0


def _fox_selectors():
    width = FOX_HEADS * LANES
    sel_k = np.zeros((3, LANES, width), np.float32)
    sel_q = np.zeros((3, LANES, width), np.float32)
    const_k = np.zeros((1, width), np.float32)
    const_q = np.zeros((1, width), np.float32)
    for hd in range(FOX_HEADS):
        base = hd * LANES + _fox_extra_base(hd)
        for part in range(3):
            sel_k[part, hd, base + part] = 1.0
            const_q[0, base + part] = -1.0
            const_k[0, base + 3 + part] = 1.0
            sel_q[part, hd, base + 3 + part] = 1.0
    return jnp.asarray(sel_k, BF16), jnp.asarray(sel_q, BF16), jnp.asarray(const_k), jnp.asarray(const_q)


def _fox_proj_kernel(x_ref, g_ref, wq_ref, wk_ref, wv_ref, wf_ref, bf_ref, selk_ref, selq_ref, ck_ref, cq_ref,
                     q_ref, k_ref, v_ref, lf_ref, kaug_ref, qaugt_ref, vt_ref, carry_ref, *, tiles_per_seq):
    tm = x_ref.shape[0]

    @pl.when(pl.program_id(0) % tiles_per_seq == 0)
    def _():
        carry_ref[...] = jnp.zeros_like(carry_ref)

    h = _rms(x_ref[...], g_ref[...]).astype(BF16)
    q = _dot(h, wq_ref[...]) * (FOX_HD ** -0.5)
    q_ref[...] = q
    k = _dot(h, wk_ref[...])
    k_ref[...] = k
    v = _dot(h, wv_ref[...])
    v_ref[...] = v
    vt_ref[...] = v.T.astype(BF16)
    lane = lax.broadcasted_iota(jnp.int32, (tm, LANES), 1)
    lf = jnp.where(lane < FOX_HEADS, _log_sigmoid(_dot(h, wf_ref[...]) + bf_ref[...]), 0.0)
    lf_ref[...] = lf[:, :FOX_HEADS]
    rows = lax.broadcasted_iota(jnp.int32, (tm, tm), 0)
    cols = lax.broadcasted_iota(jnp.int32, (tm, tm), 1)
    c = _dot((rows >= cols).astype(F32), lf, precision=HIGHEST) + carry_ref[...]
    carry_ref[...] = c[tm - 1:tm, :]
    c2 = c * LOG2E
    hi = c2.astype(BF16)
    r1 = c2 - hi.astype(F32)
    mid = r1.astype(BF16)
    lo = (r1 - mid.astype(F32)).astype(BF16)
    ext_k = _dot(hi, selk_ref[0]) + _dot(mid, selk_ref[1]) + _dot(lo, selk_ref[2]) + ck_ref[...]
    ext_q = _dot(hi, selq_ref[0]) + _dot(mid, selq_ref[1]) + _dot(lo, selq_ref[2]) + cq_ref[...]
    for hd in range(FOX_HEADS):
        src = slice((hd // 2) * LANES, (hd // 2 + 1) * LANES)
        dst = slice(hd * LANES, (hd + 1) * LANES)
        own = (lane < FOX_HD) if hd % 2 == 0 else (lane >= FOX_HD)
        kaug_ref[:, dst] = jnp.where(own, k[:, src], ext_k[:, dst]).astype(BF16)
        qaugt_ref[dst, :] = jnp.where(own, q[:, src] * LOG2E, ext_q[:, dst]).T.astype(BF16)


def _fox_proj(x, g, wq, wk, wv, wf, bf, tm, seq):
    n, d = x.shape
    width = FOX_HEADS * LANES
    tok = pl.BlockSpec((tm, d), lambda i: (i, 0))
    sel_k, sel_q, const_k, const_q = _fox_selectors()
    return pl.pallas_call(
        functools.partial(_fox_proj_kernel, tiles_per_seq=max(seq // tm, 1)),
        grid=(n // tm,),
        in_specs=[tok, _const_spec((1, d)), _const_spec((d, d)), _const_spec((d, d)), _const_spec((d, d)),
                  _const_spec((d, LANES)), _const_spec((1, LANES)), _const_spec((3, LANES, width)),
                  _const_spec((3, LANES, width)), _const_spec((1, width)), _const_spec((1, width))],
        out_specs=[tok, tok, tok, pl.BlockSpec((tm, FOX_HEADS), lambda i: (i, 0)),
                   pl.BlockSpec((tm, width), lambda i: (i, 0)), pl.BlockSpec((width, tm), lambda i: (0, i)),
                   pl.BlockSpec((d, tm), lambda i: (0, i))],
        out_shape=[jax.ShapeDtypeStruct((n, d), F32), jax.ShapeDtypeStruct((n, d), F32),
                   jax.ShapeDtypeStruct((n, d), F32), jax.ShapeDtypeStruct((n, FOX_HEADS), F32),
                   jax.ShapeDtypeStruct((n, width), BF16), jax.ShapeDtypeStruct((width, n), BF16),
                   jax.ShapeDtypeStruct((d, n), BF16)],
        scratch_shapes=[pltpu.VMEM((1, LANES), F32)],
        compiler_params=_params(1),
        name="fox_proj",
    )(x, g, wq, wk, wv, wf, bf, sel_k, sel_q, const_k, const_q)


def _fox_flash_kernel(qi_ref, ki_ref, qt_ref, k_ref, vt_ref, o_ref, s_ref, m_ref, l_ref, acc_ref, *, tq, tk):
    t = pl.program_id(2)
    qi = qi_ref[t]
    ki = ki_ref[t]

    @pl.when(ki == 0)
    def _():
        m_ref[...] = jnp.full_like(m_ref, -jnp.inf)
        l_ref[...] = jnp.zeros_like(l_ref)
        acc_ref[...] = jnp.zeros_like(acc_ref)

    def step(masked):
        if masked:
            keep = (lax.broadcasted_iota(jnp.int32, (tk, tq), 0) <= lax.broadcasted_iota(jnp.int32, (tk, tq), 1))
        for hh in range(2):
            feat = slice(hh * LANES, (hh + 1) * LANES)
            s = _dot(k_ref[:, feat], qt_ref[feat, :])
            if masked:
                s = jnp.where(keep, s, -jnp.inf)
            s_ref[hh] = s
            m_prev = m_ref[hh]
            m_new = jnp.maximum(m_prev, jnp.max(s, axis=0, keepdims=True))
            alpha = jnp.exp2(m_prev - m_new)
            p = jnp.exp2(s_ref[hh] - m_new)
            l_ref[hh] = alpha * l_ref[hh] + jnp.sum(p, axis=0, keepdims=True)
            m_ref[hh] = m_new
            acc_ref[hh] = acc_ref[hh] * alpha + _dot(vt_ref[hh * FOX_HD:(hh + 1) * FOX_HD, :], p.astype(BF16))

    @pl.when(ki < qi)
    def _():
        step(False)

    @pl.when(ki == qi)
    def _():
        step(True)
        for hh in range(2):
            o_ref[hh * FOX_HD:(hh + 1) * FOX_HD, :] = acc_ref[hh] / l_ref[hh]


def _fox_flash(qaugt, kaug, vt, batch, tq):
    d, n = vt.shape
    s = n // batch
    nq = s // tq
    tri = [(qi, ki) for qi in range(nq) for ki in range(qi + 1)]
    qi_of = jnp.array([a for a, _ in tri], jnp.int32)
    ki_of = jnp.array([b for _, b in tri], jnp.int32)
    grid_spec = pltpu.PrefetchScalarGridSpec(
        num_scalar_prefetch=2,
        grid=(batch, FOX_PAIRS, len(tri)),
        in_specs=[pl.BlockSpec((2 * LANES, tq), lambda b, pr, t, qi, ki: (pr, b * nq + qi[t])),
                  pl.BlockSpec((tq, 2 * LANES), lambda b, pr, t, qi, ki: (b * nq + ki[t], pr)),
                  pl.BlockSpec((LANES, tq), lambda b, pr, t, qi, ki: (pr, b * nq + ki[t]))],
        out_specs=pl.BlockSpec((LANES, tq), lambda b, pr, t, qi, ki: (pr, b * nq + qi[t])),
        scratch_shapes=[pltpu.VMEM((2, tq, tq), F32), pltpu.VMEM((2, 1, tq), F32), pltpu.VMEM((2, 1, tq), F32),
                        pltpu.VMEM((2, FOX_HD, tq), F32)],
    )
    return pl.pallas_call(
        functools.partial(_fox_flash_kernel, tq=tq, tk=tq),
        grid_spec=grid_spec,
        out_shape=jax.ShapeDtypeStruct((d, n), F32),
        compiler_params=_params(3),
        name="fox_flash",
    )(qi_of, ki_of, qaugt, kaug, vt)


def _fox_sample_kernel(pt_ref, q_ref, lfc_ref, kn_ref, vn_ref, lfn_ref, *rest, pages_per_step, t_new):
    k_refs = rest[:pages_per_step]
    v_refs = rest[pages_per_step:2 * pages_per_step]
    lf_refs = rest[2 * pages_per_step:3 * pages_per_step]
    o_ref, cnb_ref, m_ref, l_ref, acc_ref, tot_ref = rest[3 * pages_per_step:]
    step = pl.program_id(1)
    t_pad = q_ref.shape[2]
    page = lf_refs[0].shape[-1]
    rows_i = lax.broadcasted_iota(jnp.int32, (page, page), 0)
    cols_i = lax.broadcasted_iota(jnp.int32, (page, page), 1)

    q = q_ref[0].astype(BF16)

    def per_head_rows(a):
        return jnp.stack([jnp.broadcast_to(a[h:h + 1, :], (t_pad, page)) for h in range(FOX_HEADS)], axis=0)

    def attend(blocks):
        s = [lax.dot_general(q, kt.astype(BF16), (((2,), (1,)), ((0,), (0,))), preferred_element_type=F32) + bias
             for kt, _, bias in blocks]
        m_prev = m_ref[...]
        m_new = m_prev
        for si in s:
            m_new = jnp.maximum(m_new, jnp.max(si, axis=2, keepdims=True))
        alpha = jnp.exp(m_prev - m_new)
        l_new = alpha * l_ref[...]
        acc = acc_ref[...] * alpha[:, :, :FOX_HD]
        for si, (_, vt, _) in zip(s, blocks):
            p = jnp.exp(si - m_new)
            l_new = l_new + jnp.sum(p, axis=2, keepdims=True)
            acc = acc + lax.dot_general(p.astype(BF16), vt.astype(BF16), (((2,), (2,)), ((0,), (0,))),
                                        preferred_element_type=F32)
        l_ref[...] = l_new
        m_ref[...] = m_new
        acc_ref[...] = acc

    @pl.when(step == 0)
    def _():
        m_ref[...] = jnp.full_like(m_ref, -jnp.inf)
        l_ref[...] = jnp.zeros_like(l_ref)
        acc_ref[...] = jnp.zeros_like(acc_ref)
        tot_ref[...] = jnp.zeros_like(tot_ref)
        col = lfc_ref[0]
        row_t = lax.broadcasted_iota(jnp.int32, (1, t_pad, 1), 1)
        run = jnp.zeros((FOX_HEADS, 1, 1), F32)
        cn = jnp.zeros((FOX_HEADS, t_pad, 1), F32)
        for t in range(t_new):
            run = run + col[:, t:t + 1, :]
            cn = jnp.where(row_t == t, run, cn)
        cnb_ref[...] = jnp.broadcast_to(cn, cnb_ref.shape)
        cnt = _dot(lfn_ref[0], (rows_i <= cols_i).astype(F32), precision=HIGHEST)
        causal = (lax.broadcasted_iota(jnp.int32, (1, t_pad, page), 2)
                  <= lax.broadcasted_iota(jnp.int32, (1, t_pad, page), 1))
        attend([(kn_ref[0], vn_ref[0], jnp.where(causal, cnb_ref[...] - per_head_rows(cnt), -jnp.inf))])

    strict = (rows_i > cols_i).astype(F32)
    ones = jnp.ones((page, page), F32)
    tot = tot_ref[...]
    blocks = []
    for i in range(pages_per_step):
        lft = lf_refs[i][0]
        r = _dot(lft, strict, precision=HIGHEST) + tot
        tot = tot + _dot(lft, ones, precision=HIGHEST)
        blocks.append((k_refs[i][0, 0], v_refs[i][0, 0], cnb_ref[...] + per_head_rows(r)))
    tot_ref[...] = tot
    attend(blocks)

    @pl.when(step == pl.num_programs(1) - 1)
    def _():
        o_ref[0] = acc_ref[...] / l_ref[...][:, :, :FOX_HD]


def _fox_sample(q, lf_col, k_new, v_new, lft_new, cache_k, cache_v, cache_lft, page_table, layer, pages_per_step):
    db, _, t_pad, _ = q.shape
    t_new = lf_col.shape[2]
    n_pages = page_table.shape[1]
    page = cache_k.shape[-1]
    assert page == LANES and cache_k.shape[2:4] == (FOX_HEADS, FOX_HD)
    n_steps = n_pages // pages_per_step
    lf_col = jnp.pad(lf_col, ((0, 0), (0, 0), (0, t_pad - t_new), (0, 0)))

    def page_of(i):
        return lambda b, s, pt: pt[b, n_pages - 1 - (s * pages_per_step + i)]

    def kv_map(i):
        idx = page_of(i)
        return lambda b, s, pt: (layer, idx(b, s, pt), 0, 0, 0)

    def lf_map(i):
        idx = page_of(i)
        return lambda b, s, pt: (idx(b, s, pt), 0, 0)

    per_b = lambda shape: pl.BlockSpec((1,) + shape, lambda b, s, pt: (b,) + (0,) * len(shape))
    kv_blk = (FOX_HEADS, FOX_HD, page)
    in_specs = [per_b((FOX_HEADS, t_pad, FOX_HD)), per_b((FOX_HEADS, t_pad, 1)), per_b(kv_blk), per_b(kv_blk),
                per_b((FOX_HEADS, page))]
    in_specs += [pl.BlockSpec((1, 1) + kv_blk, kv_map(i)) for i in range(pages_per_step)]
    in_specs += [pl.BlockSpec((1, 1) + kv_blk, kv_map(i)) for i in range(pages_per_step)]
    in_specs += [pl.BlockSpec((1, FOX_HEADS, page), lf_map(i)) for i in range(pages_per_step)]
    grid_spec = pltpu.PrefetchScalarGridSpec(
        num_scalar_prefetch=1,
        grid=(db, n_steps),
        in_specs=in_specs,
        out_specs=per_b((FOX_HEADS, t_pad, FOX_HD)),
        scratch_shapes=[pltpu.VMEM((FOX_HEADS, t_pad, LANES), F32), pltpu.VMEM((FOX_HEADS, t_pad, LANES), F32),
                        pltpu.VMEM((FOX_HEADS, t_pad, LANES), F32), pltpu.VMEM((FOX_HEADS, t_pad, FOX_HD), F32),
                        pltpu.VMEM((FOX_HEADS, page), F32)],
    )
    return pl.pallas_call(
        functools.partial(_fox_sample_kernel, pages_per_step=pages_per_step, t_new=t_new),
        grid_spec=grid_spec,
        out_shape=jax.ShapeDtypeStruct((db, FOX_HEADS, t_pad, FOX_HD), F32),
        compiler_params=_params(2),
        name="fox_sample",
    )(page_table, q, lf_col, k_new, v_new, lft_new,
      *([cache_k] * pages_per_step), *([cache_v] * pages_per_step), *([cache_lft] * pages_per_step))


def _tile(n, pref):
    return pref if n % pref == 0 else n


def _peer(x1, xnt, wpqt, keys, u, vt, gfin, final_norm):
    n = x1.shape[0]
    s1, s2, tau = _peer_route(xnt, wpqt, keys, _tile(n, 512))
    return _peer_dense(xnt, u, vt, s1, s2, tau, x1, gfin, _tile(n, 1024), 512, final_norm)


def kernel(x_prompt, x_sample, state_gla, cache_k, cache_v, cache_logf, page_table, g_mix, g_ffn, g_final, gla_w_q, gla_w_k, gla_w_v, gla_w_g1, gla_w_g2, gla_b_g, gla_w_r, gla_b_r, gla_g_o, gla_w_o, fox_w_q, fox_w_k, fox_w_v, fox_w_f, fox_b_f, fox_w_o, peer_w_q, peer_keys, peer_u, peer_v):
    batch, seq, d = x_prompt.shape
    db, t_new, _ = x_sample.shape
    n_p, n_s = batch * seq, db * t_new
    xp = x_prompt.reshape(n_p, d)
    xs = x_sample.reshape(n_s, d)
    row = lambda a: a.reshape(1, -1).astype(F32)
    gfin = row(g_final)

    peer = []
    for i in range(peer_w_q.shape[0]):
        peer.append((peer_w_q[i].T.astype(BF16),
                     peer_keys[i].reshape(2 * PEER_HEADS, PEER_NKEYS, PEER_NKEYS).astype(BF16),
                     peer_u[i].astype(BF16), peer_v[i].T.astype(BF16)))

    rank = gla_w_g1.shape[-1]
    wg1 = jnp.pad(gla_w_g1[0], ((0, 0), (0, LANES - rank))).astype(BF16)
    wg2 = jnp.pad(gla_w_g2[0], ((0, LANES - rank), (0, 0))).astype(BF16)
    gla_w = (row(g_mix[0]), gla_w_q[0].astype(BF16), gla_w_k[0].astype(BF16), gla_w_v[0].astype(BF16),
             gla_w_r[0].astype(BF16), wg1, wg2, row(gla_b_g[0]), row(gla_b_r[0]))
    wo0 = gla_w_o[0].astype(BF16)

    def gla_layer(x, b, s, s0t, chunk, sub, pad_to):
        n = x.shape[0]
        q, k, v, lg, r = _gla_proj(x, *gla_w, _tile(n, 512))
        shp = lambda a: a.reshape(b, s, a.shape[-1])
        q, k, v, lg = shp(q), shp(k), shp(v), shp(lg)
        if pad_to > s:
            padseq = lambda a: jnp.pad(a, ((0, 0), (0, pad_to - s), (0, 0)))
            q, k, v, lg = padseq(q), padseq(k), padseq(v), padseq(lg)
        o, sfin_t = _gla_scan(q, k, v, lg, s0t, chunk, sub)
        o = o[:, :s].reshape(n, -1)
        x1, xnt = _post_mix(o, r, row(gla_g_o[0]), wo0, x, row(g_ffn[0]), _tile(n, 512))
        return _peer(x1, xnt, *peer[0], gfin, False), jnp.swapaxes(sfin_t, -1, -2)

    s0p = jnp.zeros((batch, GLA_HEADS, GLA_DV, GLA_DK), F32)
    xp, gla_sp = gla_layer(xp, batch, seq, s0p, GLA_CHUNK, GLA_SUB, seq)
    s0s = jnp.swapaxes(state_gla[0], -1, -2).astype(F32)
    xs, gla_ss = gla_layer(xs, db, t_new, s0s, 8, 8, 8)

    wf = jnp.pad(fox_w_f[0], ((0, 0), (0, LANES - FOX_HEADS))).astype(BF16)
    bf = jnp.pad(fox_b_f[0], (0, LANES - FOX_HEADS)).reshape(1, LANES).astype(F32)
    fox_w = (row(g_mix[1]), fox_w_q[0].astype(BF16), fox_w_k[0].astype(BF16), fox_w_v[0].astype(BF16), wf, bf)
    wo1 = fox_w_o[0].astype(BF16)

    _, kp, vp, lfp, kaug, qaugt, vpt = _fox_proj(xp, *fox_w, _tile(n_p, 512), seq)
    opt = _fox_flash(qaugt, kaug, vpt, batch, _tile(seq, 1024))
    x1, xnt = _post_mix(opt, None, None, wo1, xp, row(g_ffn[1]), _tile(n_p, 512))
    yp = _peer(x1, xnt, *peer[1], gfin, True)

    qs, ks, vs, lfs, _, _, _ = _fox_proj(xs, *fox_w, _tile(n_s, 512), t_new)
    page = cache_k.shape[2]
    t_pad = 8
    heads_major = lambda a: a.reshape(db, t_new, FOX_HEADS, -1).transpose(0, 2, 1, 3)
    q_h = jnp.pad(heads_major(qs), ((0, 0), (0, 0), (0, t_pad - t_new), (0, 0)))
    lf_h = heads_major(lfs)
    keys_last = lambda a: jnp.pad(heads_major(a).transpose(0, 1, 3, 2), ((0, 0), (0, 0), (0, 0), (0, page - t_new)))
    k_new, v_new = keys_last(ks), keys_last(vs)
    lft_new = jnp.pad(lf_h[..., 0], ((0, 0), (0, 0), (0, page - t_new)))
    ckt = cache_k.transpose(0, 1, 3, 4, 2)
    cvt = cache_v.transpose(0, 1, 3, 4, 2)
    clft = jnp.swapaxes(cache_logf[0], -1, -2)
    os_h = _fox_sample(q_h, lf_h, k_new, v_new, lft_new, ckt, cvt, clft, page_table, 0, 8)
    ost = os_h[:, :, :t_new, :].transpose(1, 3, 0, 2).reshape(d, n_s)
    x1, xnt = _post_mix(ost, None, None, wo1, xs, row(g_ffn[1]), _tile(n_s, 512))
    ys = _peer(x1, xnt, *peer[1], gfin, True)

    return (yp.reshape(batch, seq, d), ys.reshape(db, t_new, d), gla_sp[None], gla_ss[None],
            kp.reshape(1, batch, seq, FOX_HEADS, FOX_HD), vp.reshape(1, batch, seq, FOX_HEADS, FOX_HD),
            lfp.reshape(1, batch, seq, FOX_HEADS),
            ks.reshape(1, db, t_new, FOX_HEADS, FOX_HD), vs.reshape(1, db, t_new, FOX_HEADS, FOX_HD),
            lfs.reshape(1, db, t_new, FOX_HEADS))
```

```python
import functools

import numpy as np
import jax
import jax.numpy as jnp
from jax import lax
from jax.experimental import pallas as pl
from jax.experimental.pallas import tpu as pltpu

F32 = jnp.float32
BF16 = jnp.bfloat16
HIGHEST = lax.Precision.HIGHEST

D_MODEL = 1024
GLA_HEADS = 4
GLA_DK = 128
GLA_DV = 256
GLA_TAU = 16.0
GLA_CHUNK = 64
GLA_SUB = 16
FOX_HEADS = 16
FOX_HD = 64
FOX_PAIRS = FOX_HEADS * FOX_HD // 128
PEER_HEADS = 8
PEER_NKEYS = 128
PEER_TOPK = 16
PEER_N_EXPERTS = PEER_NKEYS * PEER_NKEYS
RMS_EPS = 1e-6
LOG2E = 1.4426950408889634
LANES = 128
VMEM_LIMIT = 56 * 1024 * 1024


def _params(n_axes, vmem=VMEM_LIMIT):
    return pltpu.CompilerParams(dimension_semantics=("arbitrary",) * n_axes, vmem_limit_bytes=vmem)


def _rms(x, g):
    return x * lax.rsqrt(jnp.mean(x * x, axis=-1, keepdims=True) + RMS_EPS) * g


def _dot(a, b, precision=None):
    return jnp.dot(a, b, preferred_element_type=F32, precision=precision)


def _dot_nt(a, b):
    return lax.dot_general(a, b, (((1,), (1,)), ((), ())), preferred_element_type=F32)


def _dot_tn(a, b):
    return lax.dot_general(a, b, (((0,), (0,)), ((), ())), preferred_element_type=F32)


def _log_sigmoid(z):
    return jnp.minimum(z, 0.0) - jnp.log1p(jnp.exp(-jnp.abs(z)))


def _const_spec(shape):
    return pl.BlockSpec(shape, lambda *_: (0,) * len(shape))


def _gla_proj_kernel(x_ref, g_ref, wq_ref, wk_ref, wv_ref, wr_ref, wg1_ref, wg2_ref, bg_ref, br_ref,
                     q_ref, k_ref, v_ref, lg_ref, r_ref):
    h = _rms(x_ref[...], g_ref[...]).astype(BF16)
    q_ref[...] = _dot(h, wq_ref[...]) * (GLA_DK ** -0.5)
    k_ref[...] = _dot(h, wk_ref[...])
    v_ref[...] = _dot(h, wv_ref[...])
    u = _dot(h, wr_ref[...]) + br_ref[...]
    r_ref[...] = u / (1.0 + jnp.exp(-u))
    t = _dot(h, wg1_ref[...]).astype(BF16)
    z = _dot(t, wg2_ref[...]) + bg_ref[...]
    lg_ref[...] = _log_sigmoid(z) * (1.0 / GLA_TAU)


def _gla_proj(x, g, wq, wk, wv, wr, wg1, wg2, bg, br, tm):
    n, d = x.shape
    gk = GLA_HEADS * GLA_DK
    gv = GLA_HEADS * GLA_DV
    tok = lambda w: pl.BlockSpec((tm, w), lambda i: (i, 0))
    return pl.pallas_call(
        _gla_proj_kernel,
        grid=(n // tm,),
        in_specs=[tok(d), _const_spec((1, d)), _const_spec((d, gk)), _const_spec((d, gk)), _const_spec((d, gv)),
                  _const_spec((d, gv)), _const_spec((d, LANES)), _const_spec((LANES, gk)), _const_spec((1, gk)),
                  _const_spec((1, gv))],
        out_specs=[tok(gk), tok(gk), tok(gv), tok(gk), tok(gv)],
        out_shape=[jax.ShapeDtypeStruct((n, gk), F32), jax.ShapeDtypeStruct((n, gk), F32),
                   jax.ShapeDtypeStruct((n, gv), F32), jax.ShapeDtypeStruct((n, gk), F32),
                   jax.ShapeDtypeStruct((n, gv), F32)],
        compiler_params=_params(1),
        name="gla_proj",
    )(x, g, wq, wk, wv, wr, wg1, wg2, bg, br)


def _gla_scan_kernel(q_ref, k_ref, v_ref, g_ref, s0_ref, o_ref, sfin_ref, st_ref, *, chunk, sub):
    j = pl.program_id(1)

    @pl.when(j == 0)
    def _():
        st_ref[...] = s0_ref[0]

    rows = lax.broadcasted_iota(jnp.int32, (chunk, chunk), 0)
    cols = lax.broadcasted_iota(jnp.int32, (chunk, chunk), 1)
    tril = (rows >= cols).astype(F32)
    t_idx = lax.broadcasted_iota(jnp.int32, (sub, sub, 1), 0)
    s_idx = lax.broadcasted_iota(jnp.int32, (sub, sub, 1), 1)
    causal = t_idx >= s_idx
    ones = jnp.ones((LANES, LANES), BF16)
    n_sub = chunk // sub
    lane3 = lax.broadcasted_iota(jnp.int32, (sub, sub, LANES), 2)
    s_idx3 = lax.broadcasted_iota(jnp.int32, (sub, sub, LANES), 1)

    for h in range(GLA_HEADS):
        ks = slice(h * GLA_DK, (h + 1) * GLA_DK)
        vs = slice(h * GLA_DV, (h + 1) * GLA_DV)
        qh = q_ref[0, :, ks]
        kh = k_ref[0, :, ks]
        vh = v_ref[0, :, vs]
        b = _dot(tril, g_ref[0, :, ks], precision=HIGHEST)
        b_last = b[chunk - 1:chunk, :]
        st = st_ref[h]
        o = _dot_nt((qh * jnp.exp(b)).astype(BF16), st.astype(BF16))
        parts = []
        for i in range(n_sub):
            bt = b[i * sub:(i + 1) * sub]
            qi = qh[i * sub:(i + 1) * sub]
            att2d = jnp.zeros((sub, LANES), F32)
            for jj in range(i + 1):
                bs = b[jj * sub:(jj + 1) * sub]
                kj = kh[jj * sub:(jj + 1) * sub]
                diff = bt[:, None, :] - bs[None, :, :]
                if jj == i:
                    diff = jnp.where(causal, diff, -jnp.inf)
                a = (qi[:, None, :] * kj[None, :, :]) * jnp.exp(diff)
                att = _dot(a.reshape(sub * sub, GLA_DK).astype(BF16), ones).reshape(sub, sub, LANES)
                att2d = att2d + jnp.sum(jnp.where(lane3 == s_idx3 + jj * sub, att, 0.0), axis=1)
            parts.append(_dot(att2d[:, :chunk], vh))
        o_ref[0, :, vs] = o + jnp.concatenate(parts, axis=0)
        kd = (kh * jnp.exp(b_last - b)).astype(BF16)
        st_ref[h] = st * jnp.exp(b_last) + _dot_tn(vh.astype(BF16), kd)

    @pl.when(j == pl.num_programs(1) - 1)
    def _():
        sfin_ref[0] = st_ref[...]


def _gla_scan(q, k, v, lg, s0t, chunk, sub):
    b, s, gk = q.shape
    gv = v.shape[-1]
    seq = lambda w: pl.BlockSpec((1, chunk, w), lambda bi, j: (bi, j, 0))
    st_spec = pl.BlockSpec((1, GLA_HEADS, GLA_DV, GLA_DK), lambda bi, j: (bi, 0, 0, 0))
    return pl.pallas_call(
        functools.partial(_gla_scan_kernel, chunk=chunk, sub=sub),
        grid=(b, s // chunk),
        in_specs=[seq(gk), seq(gk), seq(gv), seq(gk), st_spec],
        out_specs=[seq(gv), st_spec],
        out_shape=[jax.ShapeDtypeStruct((b, s, gv), F32),
                   jax.ShapeDtypeStruct((b, GLA_HEADS, GLA_DV, GLA_DK), F32)],
        scratch_shapes=[pltpu.VMEM((GLA_HEADS, GLA_DV, GLA_DK), F32)],
        compiler_params=_params(2),
        name="gla_scan",
    )(q, k, v, lg, s0t)


def _post_mix_kernel(*refs, gla):
    if gla:
        o_ref, r_ref, go_ref, wo_ref, x_ref, gf_ref, x1_ref, xnt_ref = refs
        parts = []
        for h in range(GLA_HEADS):
            vs = slice(h * GLA_DV, (h + 1) * GLA_DV)
            parts.append(_rms(o_ref[:, vs], go_ref[...]) * r_ref[:, vs])
        a = jnp.concatenate(parts, axis=1)
    else:
        o_ref, wo_ref, x_ref, gf_ref, x1_ref, xnt_ref = refs
        a = o_ref[...].T
    x1 = x_ref[...] + _dot(a.astype(BF16), wo_ref[...])
    x1_ref[...] = x1
    xnt_ref[...] = _rms(x1, gf_ref[...]).T.astype(BF16)


def _post_mix(o, r, go, wo, x, gf, tm):
    n, d = x.shape
    tok = pl.BlockSpec((tm, d), lambda i: (i, 0))
    gla = r is not None
    if gla:
        args = (o, r, go, wo, x, gf)
        in_specs = [tok, tok, _const_spec((1, GLA_DV)), _const_spec((d, d)), tok, _const_spec((1, d))]
    else:
        args = (o, wo, x, gf)
        in_specs = [pl.BlockSpec((d, tm), lambda i: (0, i)), _const_spec((d, d)), tok, _const_spec((1, d))]
    return pl.pallas_call(
        functools.partial(_post_mix_kernel, gla=gla),
        grid=(n // tm,),
        in_specs=in_specs,
        out_specs=[tok, pl.BlockSpec((d, tm), lambda i: (0, i))],
        out_shape=[jax.ShapeDtypeStruct((n, d), F32), jax.ShapeDtypeStruct((d, n), BF16)],
        compiler_params=_params(1),
        name="post_mix_gla" if gla else "post_mix_fox",
    )(*args)


_PEER_CAND_ROWS = 80


def _peer_candidates(v1, v2):
    r8 = lax.broadcasted_iota(jnp.int32, (8, 1), 0)
    pieces = []
    for a in range(8):
        z = v1[a:a + 1] + v2[0:8]
        pieces.append(z if a == 0 else jnp.where(r8 < PEER_TOPK // (a + 1), z, -jnp.inf))
    pieces.append(v1[0:1] + v2[8:16])
    pieces.append(v1[8:16] + v2[0:1])
    return jnp.concatenate(pieces, axis=0)


def _peer_route_kernel(xnt_ref, wpqt_ref, keys_ref, s1_ref, s2_ref, tau_ref, top_ref):
    xnt = xnt_ref[...]

    def head(p, carry):
        s = []
        for half in range(2):
            row = pl.multiple_of((2 * p + half) * PEER_NKEYS, PEER_NKEYS)
            qt = _dot(wpqt_ref[pl.ds(row, PEER_NKEYS), :], xnt).astype(BF16)
            sc = _dot(keys_ref[2 * p + half], qt)
            s.append(sc)
            cur = sc
            for rnk in range(PEER_TOPK):
                mx = jnp.max(cur, axis=0, keepdims=True)
                top_ref[half, rnk:rnk + 1, :] = mx
                cur = jnp.where(cur == mx, -jnp.inf, cur)
        v1 = top_ref[0]
        v2 = top_ref[1]
        m = v1[0:1] + v2[0:1]
        v1m = v1 - m
        z = _peer_candidates(v1m, v2)
        cur = z
        for rnk in range(PEER_TOPK):
            tau = jnp.max(cur, axis=0, keepdims=True)
            if rnk + 1 < PEER_TOPK:
                cur = jnp.where(cur == tau, -jnp.inf, cur)
        sel = z >= tau
        log_z = jnp.log(jnp.sum(jnp.where(sel, jnp.exp(z), 0.0), axis=0, keepdims=True))
        z2 = _peer_candidates((v1m - log_z) * LOG2E, v2 * LOG2E)
        tau2 = jnp.min(jnp.where(sel, z2, jnp.inf), axis=0, keepdims=True)
        row = pl.multiple_of(p * PEER_NKEYS, PEER_NKEYS)
        s1_ref[pl.ds(row, PEER_NKEYS), :] = ((s[0] - m) - log_z) * LOG2E
        s2_ref[pl.ds(row, PEER_NKEYS), :] = s[1] * LOG2E
        tau_ref[pl.ds(p, 1), :] = tau2
        return carry

    lax.fori_loop(0, PEER_HEADS, head, 0)


def _peer_route(xnt, wpqt, keys, tn):
    d, n = xnt.shape
    rows = PEER_HEADS * PEER_NKEYS
    return pl.pallas_call(
        _peer_route_kernel,
        grid=(n // tn,),
        in_specs=[pl.BlockSpec((d, tn), lambda i: (0, i)), _const_spec((2 * rows, d)),
                  _const_spec((2 * PEER_HEADS, PEER_NKEYS, PEER_NKEYS))],
        out_specs=[pl.BlockSpec((rows, tn), lambda i: (0, i)), pl.BlockSpec((rows, tn), lambda i: (0, i)),
                   pl.BlockSpec((PEER_HEADS, tn), lambda i: (0, i))],
        out_shape=[jax.ShapeDtypeStruct((rows, n), F32), jax.ShapeDtypeStruct((rows, n), F32),
                   jax.ShapeDtypeStruct((PEER_HEADS, n), F32)],
        scratch_shapes=[pltpu.VMEM((2, PEER_TOPK, tn), F32)],
        compiler_params=_params(1),
        name="peer_route",
    )(xnt, wpqt, keys)


def _peer_dense_kernel(xnt_ref, u_ref, unext_ref, vt_ref, vtprev_ref, s1_ref, s2_ref, tau_ref, x1_ref, gfin_ref,
                       out_ref, acc_ref, act_ref, act0_ref, p_ref, plast_ref, *, te, tn, lc, final_norm):
    e = pl.program_id(1)
    n_e = pl.num_programs(1)
    slot = e % 2
    other = 1 - slot
    n_i1 = te // PEER_NKEYS
    n_chunks = tn // lc
    last = n_chunks - 1
    chunk = lambda k: slice(k * lc, (k + 1) * lc)

    @pl.when(e == 0)
    def _():
        acc_ref[...] = jnp.zeros_like(acc_ref)
        act0_ref[slot] = _dot(u_ref[...], xnt_ref[:, chunk(0)])
        plast_ref[other] = jnp.zeros((te, lc), BF16)

    for k in range(n_chunks):
        cols = chunk(k)
        for il in range(n_i1):
            rows = slice(il * PEER_NKEYS, (il + 1) * PEER_NKEYS)
            i1 = e * n_i1 + il
            w = None
            for p in range(PEER_HEADS):
                z = s2_ref[p * PEER_NKEYS:(p + 1) * PEER_NKEYS, cols] + s1_ref[pl.ds(p * PEER_NKEYS + i1, 1), cols]
                term = jnp.where(z >= tau_ref[p:p + 1, cols], jnp.exp2(z), 0.0)
                w = term if w is None else w + term
            a = act0_ref[slot, rows, :] if k == 0 else act_ref[rows, cols]
            gelu = 0.5 * a * (1.0 + lax.erf(a * (2.0 ** -0.5)))
            pv = (w * gelu).astype(BF16)
            if k == last:
                plast_ref[slot, rows, :] = pv
            else:
                p_ref[rows, cols] = pv
            if il == 0:
                if k < last:
                    act_ref[:, chunk(k + 1)] = _dot(u_ref[...], xnt_ref[:, chunk(k + 1)])
                else:
                    act0_ref[other] = _dot(unext_ref[...], xnt_ref[:, chunk(0)])
            if il == 0:
                if k > 0:
                    acc_ref[:, chunk(k - 1)] += _dot(vt_ref[...], p_ref[:, chunk(k - 1)])
                else:
                    acc_ref[:, chunk(last)] += _dot(vtprev_ref[...], plast_ref[other])

    @pl.when(e == n_e - 1)
    def _():
        acc_ref[:, chunk(last)] += _dot(vt_ref[...], plast_ref[slot])
        y = x1_ref[...] + acc_ref[...].T
        if final_norm:
            y = _rms(y, gfin_ref[...])
        out_ref[...] = y


def _peer_dense(xnt, u, vt, s1, s2, tau, x1, gfin, tn, te, final_norm):
    d, n = xnt.shape
    rows = PEER_HEADS * PEER_NKEYS
    lc = min(tn, 2 * LANES)
    n_e = PEER_N_EXPERTS // te
    tokT = lambda r: pl.BlockSpec((r, tn), lambda i, e: (0, i))
    tok = pl.BlockSpec((tn, d), lambda i, e: (i, 0))
    return pl.pallas_call(
        functools.partial(_peer_dense_kernel, te=te, tn=tn, lc=lc, final_norm=final_norm),
        grid=(n // tn, n_e),
        in_specs=[tokT(d),
                  pl.BlockSpec((te, d), lambda i, e: (e, 0)),
                  pl.BlockSpec((te, d), lambda i, e: (jnp.minimum(e + 1, n_e - 1), 0)),
                  pl.BlockSpec((d, te), lambda i, e: (0, e)),
                  pl.BlockSpec((d, te), lambda i, e: (0, jnp.maximum(e - 1, 0))),
                  tokT(rows), tokT(rows), tokT(PEER_HEADS), tok, pl.BlockSpec((1, d), lambda i, e: (0, 0))],
        out_specs=tok,
        out_shape=jax.ShapeDtypeStruct((n, d), F32),
        scratch_shapes=[pltpu.VMEM((d, tn), F32), pltpu.VMEM((te, tn), F32), pltpu.VMEM((2, te, lc), F32),
                        pltpu.VMEM((te, tn), BF16), pltpu.VMEM((2, te, lc), BF16)],
        compiler_params=_params(2),
        name="peer_dense",
    )(xnt, u, u, vt, vt, s1, s2, tau, x1, gfin)


_FOX_EXTRA = 6


def _fox_extra_base(head):
    return FOX_HD if head % 2 == 0 else 0


def _fox_selectors():
    width = FOX_HEADS * LANES
    sel_k = np.zeros((3, LANES, width), np.float32)
    sel_q = np.zeros((3, LANES, width), np.float32)
    const_k = np.zeros((1, width), np.float32)
    const_q = np.zeros((1, width), np.float32)
    for hd in range(FOX_HEADS):
        base = hd * LANES + _fox_extra_base(hd)
        for part in range(3):
            sel_k[part, hd, base + part] = 1.0
            const_q[0, base + part] = -1.0
            const_k[0, base + 3 + part] = 1.0
            sel_q[part, hd, base + 3 + part] = 1.0
    return jnp.asarray(sel_k, BF16), jnp.asarray(sel_q, BF16), jnp.asarray(const_k), jnp.asarray(const_q)


def _fox_proj_kernel(x_ref, g_ref, wq_ref, wk_ref, wv_ref, wf_ref, bf_ref, selk_ref, selq_ref, ck_ref, cq_ref,
                     q_ref, k_ref, v_ref, lf_ref, kaug_ref, qaugt_ref, vt_ref, carry_ref, *, tiles_per_seq,
                     kv_feature_major):
    tm = x_ref.shape[0]

    @pl.when(pl.program_id(0) % tiles_per_seq == 0)
    def _():
        carry_ref[...] = jnp.zeros_like(carry_ref)

    h = _rms(x_ref[...], g_ref[...]).astype(BF16)
    q = _dot(h, wq_ref[...]) * (FOX_HD ** -0.5)
    q_ref[...] = q
    k = _dot(h, wk_ref[...])
    v = _dot(h, wv_ref[...])
    v_t = v.T
    if kv_feature_major:
        k_ref[0] = k.T
        v_ref[0] = v_t
    else:
        k_ref[...] = k
        v_ref[...] = v
    vt_ref[...] = v_t.astype(BF16)
    lane = lax.broadcasted_iota(jnp.int32, (tm, LANES), 1)
    lf = jnp.where(lane < FOX_HEADS, _log_sigmoid(_dot(h, wf_ref[...]) + bf_ref[...]), 0.0)
    lf_ref[...] = lf[:, :FOX_HEADS]
    rows = lax.broadcasted_iota(jnp.int32, (tm, tm), 0)
    cols = lax.broadcasted_iota(jnp.int32, (tm, tm), 1)
    c = _dot((rows >= cols).astype(F32), lf, precision=HIGHEST) + carry_ref[...]
    carry_ref[...] = c[tm - 1:tm, :]
    c2 = c * LOG2E
    hi = c2.astype(BF16)
    r1 = c2 - hi.astype(F32)
    mid = r1.astype(BF16)
    lo = (r1 - mid.astype(F32)).astype(BF16)
    ext_k = _dot(hi, selk_ref[0]) + _dot(mid, selk_ref[1]) + _dot(lo, selk_ref[2]) + ck_ref[...]
    ext_q = _dot(hi, selq_ref[0]) + _dot(mid, selq_ref[1]) + _dot(lo, selq_ref[2]) + cq_ref[...]
    for hd in range(FOX_HEADS):
        src = slice((hd // 2) * LANES, (hd // 2 + 1) * LANES)
        dst = slice(hd * LANES, (hd + 1) * LANES)
        own = (lane < FOX_HD) if hd % 2 == 0 else (lane >= FOX_HD)
        kaug_ref[:, dst] = jnp.where(own, k[:, src], ext_k[:, dst]).astype(BF16)
        qaugt_ref[dst, :] = jnp.where(own, q[:, src] * LOG2E, ext_q[:, dst]).T.astype(BF16)


def _fox_proj(x, g, wq, wk, wv, wf, bf, tm, seq):
    n, d = x.shape
    width = FOX_HEADS * LANES
    tok = pl.BlockSpec((tm, d), lambda i: (i, 0))
    sel_k, sel_q, const_k, const_q = _fox_selectors()
    tiles_per_seq = max(seq // tm, 1)
    kv_feature_major = seq % tm == 0
    if kv_feature_major:
        kv_spec = pl.BlockSpec((1, d, tm), lambda i: (i // tiles_per_seq, 0, i % tiles_per_seq))
        kv_shape = jax.ShapeDtypeStruct((n // seq, d, seq), F32)
    else:
        kv_spec, kv_shape = tok, jax.ShapeDtypeStruct((n, d), F32)
    return pl.pallas_call(
        functools.partial(_fox_proj_kernel, tiles_per_seq=tiles_per_seq, kv_feature_major=kv_feature_major),
        grid=(n // tm,),
        in_specs=[tok, _const_spec((1, d)), _const_spec((d, d)), _const_spec((d, d)), _const_spec((d, d)),
                  _const_spec((d, LANES)), _const_spec((1, LANES)), _const_spec((3, LANES, width)),
                  _const_spec((3, LANES, width)), _const_spec((1, width)), _const_spec((1, width))],
        out_specs=[tok, kv_spec, kv_spec, pl.BlockSpec((tm, FOX_HEADS), lambda i: (i, 0)),
                   pl.BlockSpec((tm, width), lambda i: (i, 0)), pl.BlockSpec((width, tm), lambda i: (0, i)),
                   pl.BlockSpec((d, tm), lambda i: (0, i))],
        out_shape=[jax.ShapeDtypeStruct((n, d), F32), kv_shape, kv_shape, jax.ShapeDtypeStruct((n, FOX_HEADS), F32),
                   jax.ShapeDtypeStruct((n, width), BF16), jax.ShapeDtypeStruct((width, n), BF16),
                   jax.ShapeDtypeStruct((d, n), BF16)],
        scratch_shapes=[pltpu.VMEM((1, LANES), F32)],
        compiler_params=_params(1),
        name="fox_proj",
    )(x, g, wq, wk, wv, wf, bf, sel_k, sel_q, const_k, const_q)


def _fox_flash_kernel(qi_ref, ki_ref, qt_ref, k_ref, vt_ref, o_ref, s_ref, m_ref, l_ref, acc_ref, *, tq, tk):
    t = pl.program_id(2)
    qi = qi_ref[t]
    ki = ki_ref[t]

    @pl.when(ki == 0)
    def _():
        m_ref[...] = jnp.full_like(m_ref, -jnp.inf)
        l_ref[...] = jnp.zeros_like(l_ref)
        acc_ref[...] = jnp.zeros_like(acc_ref)

    half = tq // 2
    parts = [(hh, c) for hh in range(2) for c in range(2)]

    def step(masked):
        def logits(i):
            hh, c = parts[i]
            feat = slice(hh * LANES, (hh + 1) * LANES)
            s = _dot(k_ref[:, feat], qt_ref[feat, c * half:(c + 1) * half])
            if masked:
                key = lax.broadcasted_iota(jnp.int32, (tk, half), 0)
                query = lax.broadcasted_iota(jnp.int32, (tk, half), 1) + c * half
                s = jnp.where(key <= query, s, -jnp.inf)
            s_ref[hh, :, c * half:(c + 1) * half] = s
            return jnp.max(s, axis=0, keepdims=True)

        def update(i, m_cur):
            hh, c = parts[i]
            cols = slice(c * half, (c + 1) * half)
            m_prev = m_ref[hh, :, cols]
            m_new = jnp.maximum(m_prev, m_cur)
            alpha = jnp.exp2(m_prev - m_new)
            p = jnp.exp2(s_ref[hh, :, cols] - m_new)
            l_ref[hh, :, cols] = alpha * l_ref[hh, :, cols] + jnp.sum(p, axis=0, keepdims=True)
            m_ref[hh, :, cols] = m_new
            acc_ref[hh, :, cols] = (acc_ref[hh, :, cols] * alpha
                                    + _dot(vt_ref[hh * FOX_HD:(hh + 1) * FOX_HD, :], p.astype(BF16)))

        m_cur = [logits(0), logits(1)]
        for i in range(len(parts)):
            update(i, m_cur[i])
            if i + 2 < len(parts):
                m_cur.append(logits(i + 2))

    @pl.when(ki < qi)
    def _():
        step(False)

    @pl.when(ki == qi)
    def _():
        step(True)
        for hh in range(2):
            o_ref[hh * FOX_HD:(hh + 1) * FOX_HD, :] = acc_ref[hh] / l_ref[hh]


def _fox_flash(qaugt, kaug, vt, batch, tq):
    d, n = vt.shape
    s = n // batch
    nq = s // tq
    tri = [(qi, ki) for qi in range(nq) for ki in range(qi + 1)]
    qi_of = jnp.array([a for a, _ in tri], jnp.int32)
    ki_of = jnp.array([b for _, b in tri], jnp.int32)
    grid_spec = pltpu.PrefetchScalarGridSpec(
        num_scalar_prefetch=2,
        grid=(batch, FOX_PAIRS, len(tri)),
        in_specs=[pl.BlockSpec((2 * LANES, tq), lambda b, pr, t, qi, ki: (pr, b * nq + qi[t])),
                  pl.BlockSpec((tq, 2 * LANES), lambda b, pr, t, qi, ki: (b * nq + ki[t], pr)),
                  pl.BlockSpec((LANES, tq), lambda b, pr, t, qi, ki: (pr, b * nq + ki[t]))],
        out_specs=pl.BlockSpec((LANES, tq), lambda b, pr, t, qi, ki: (pr, b * nq + qi[t])),
        scratch_shapes=[pltpu.VMEM((2, tq, tq), F32), pltpu.VMEM((2, 1, tq), F32), pltpu.VMEM((2, 1, tq), F32),
                        pltpu.VMEM((2, FOX_HD, tq), F32)],
    )
    return pl.pallas_call(
        functools.partial(_fox_flash_kernel, tq=tq, tk=tq),
        grid_spec=grid_spec,
        out_shape=jax.ShapeDtypeStruct((d, n), F32),
        compiler_params=_params(3),
        name="fox_flash",
    )(qi_of, ki_of, qaugt, kaug, vt)


def _fox_sample_kernel(pt_ref, q_ref, lfc_ref, kn_ref, vn_ref, lfn_ref, *rest, pages_per_step, t_new):
    k_refs = rest[:pages_per_step]
    v_refs = rest[pages_per_step:2 * pages_per_step]
    lf_refs = rest[2 * pages_per_step:3 * pages_per_step]
    o_ref, cnb_ref, m_ref, l_ref, acc_ref, tot_ref = rest[3 * pages_per_step:]
    step = pl.program_id(1)
    t_pad = q_ref.shape[2]
    page = lf_refs[0].shape[-1]
    rows_i = lax.broadcasted_iota(jnp.int32, (page, page), 0)
    cols_i = lax.broadcasted_iota(jnp.int32, (page, page), 1)

    q = q_ref[0].astype(BF16)

    def per_head_rows(a):
        return jnp.stack([jnp.broadcast_to(a[h:h + 1, :], (t_pad, page)) for h in range(FOX_HEADS)], axis=0)

    def attend(blocks):
        s = [lax.dot_general(q, kt.astype(BF16), (((2,), (1,)), ((0,), (0,))), preferred_element_type=F32) + bias
             for kt, _, bias in blocks]
        m_prev = m_ref[...]
        m_new = m_prev
        for si in s:
            m_new = jnp.maximum(m_new, jnp.max(si, axis=2, keepdims=True))
        alpha = jnp.exp(m_prev - m_new)
        l_new = alpha * l_ref[...]
        acc = acc_ref[...] * alpha[:, :, :FOX_HD]
        for si, (_, vt, _) in zip(s, blocks):
            p = jnp.exp(si - m_new)
            l_new = l_new + jnp.sum(p, axis=2, keepdims=True)
            acc = acc + lax.dot_general(p.astype(BF16), vt.astype(BF16), (((2,), (2,)), ((0,), (0,))),
                                        preferred_element_type=F32)
        l_ref[...] = l_new
        m_ref[...] = m_new
        acc_ref[...] = acc

    @pl.when(step == 0)
    def _():
        m_ref[...] = jnp.full_like(m_ref, -jnp.inf)
        l_ref[...] = jnp.zeros_like(l_ref)
        acc_ref[...] = jnp.zeros_like(acc_ref)
        tot_ref[...] = jnp.zeros_like(tot_ref)
        col = lfc_ref[0]
        row_t = lax.broadcasted_iota(jnp.int32, (1, t_pad, 1), 1)
        run = jnp.zeros((FOX_HEADS, 1, 1), F32)
        cn = jnp.zeros((FOX_HEADS, t_pad, 1), F32)
        for t in range(t_new):
            run = run + col[:, t:t + 1, :]
            cn = jnp.where(row_t == t, run, cn)
        cnb_ref[...] = jnp.broadcast_to(cn, cnb_ref.shape)
        cnt = _dot(lfn_ref[0], (rows_i <= cols_i).astype(F32), precision=HIGHEST)
        causal = (lax.broadcasted_iota(jnp.int32, (1, t_pad, page), 2)
                  <= lax.broadcasted_iota(jnp.int32, (1, t_pad, page), 1))
        attend([(kn_ref[0], vn_ref[0], jnp.where(causal, cnb_ref[...] - per_head_rows(cnt), -jnp.inf))])

    strict = (rows_i > cols_i).astype(F32)
    ones = jnp.ones((page, page), F32)
    tot = tot_ref[...]
    blocks = []
    for i in range(pages_per_step):
        lft = lf_refs[i][0]
        r = _dot(lft, strict, precision=HIGHEST) + tot
        tot = tot + _dot(lft, ones, precision=HIGHEST)
        blocks.append((k_refs[i][0, 0], v_refs[i][0, 0], cnb_ref[...] + per_head_rows(r)))
    tot_ref[...] = tot
    attend(blocks)

    @pl.when(step == pl.num_programs(1) - 1)
    def _():
        o_ref[0] = acc_ref[...] / l_ref[...][:, :, :FOX_HD]


def _fox_sample(q, lf_col, k_new, v_new, lft_new, cache_k, cache_v, cache_lft, page_table, layer, pages_per_step):
    db, _, t_pad, _ = q.shape
    t_new = lf_col.shape[2]
    n_pages = page_table.shape[1]
    page = cache_k.shape[-1]
    assert page == LANES and cache_k.shape[2:4] == (FOX_HEADS, FOX_HD)
    n_steps = n_pages // pages_per_step
    lf_col = jnp.pad(lf_col, ((0, 0), (0, 0), (0, t_pad - t_new), (0, 0)))

    def page_of(i):
        return lambda b, s, pt: pt[b, n_pages - 1 - (s * pages_per_step + i)]

    def kv_map(i):
        idx = page_of(i)
        return lambda b, s, pt: (layer, idx(b, s, pt), 0, 0, 0)

    def lf_map(i):
        idx = page_of(i)
        return lambda b, s, pt: (idx(b, s, pt), 0, 0)

    per_b = lambda shape: pl.BlockSpec((1,) + shape, lambda b, s, pt: (b,) + (0,) * len(shape))
    kv_blk = (FOX_HEADS, FOX_HD, page)
    in_specs = [per_b((FOX_HEADS, t_pad, FOX_HD)), per_b((FOX_HEADS, t_pad, 1)), per_b(kv_blk), per_b(kv_blk),
                per_b((FOX_HEADS, page))]
    in_specs += [pl.BlockSpec((1, 1) + kv_blk, kv_map(i)) for i in range(pages_per_step)]
    in_specs += [pl.BlockSpec((1, 1) + kv_blk, kv_map(i)) for i in range(pages_per_step)]
    in_specs += [pl.BlockSpec((1, FOX_HEADS, page), lf_map(i)) for i in range(pages_per_step)]
    grid_spec = pltpu.PrefetchScalarGridSpec(
        num_scalar_prefetch=1,
        grid=(db, n_steps),
        in_specs=in_specs,
        out_specs=per_b((FOX_HEADS, t_pad, FOX_HD)),
        scratch_shapes=[pltpu.VMEM((FOX_HEADS, t_pad, LANES), F32), pltpu.VMEM((FOX_HEADS, t_pad, LANES), F32),
                        pltpu.VMEM((FOX_HEADS, t_pad, LANES), F32), pltpu.VMEM((FOX_HEADS, t_pad, FOX_HD), F32),
                        pltpu.VMEM((FOX_HEADS, page), F32)],
    )
    return pl.pallas_call(
        functools.partial(_fox_sample_kernel, pages_per_step=pages_per_step, t_new=t_new),
        grid_spec=grid_spec,
        out_shape=jax.ShapeDtypeStruct((db, FOX_HEADS, t_pad, FOX_HD), F32),
        compiler_params=_params(2),
        name="fox_sample",
    )(page_table, q, lf_col, k_new, v_new, lft_new,
      *([cache_k] * pages_per_step), *([cache_v] * pages_per_step), *([cache_lft] * pages_per_step))


def _tile(n, pref):
    return pref if n % pref == 0 else n


def _peer(x1, xnt, wpqt, keys, u, vt, gfin, final_norm):
    n = x1.shape[0]
    s1, s2, tau = _peer_route(xnt, wpqt, keys, _tile(n, 512))
    return _peer_dense(xnt, u, vt, s1, s2, tau, x1, gfin, _tile(n, 1024), 512, final_norm)


def kernel(x_prompt, x_sample, state_gla, cache_k, cache_v, cache_logf, page_table, g_mix, g_ffn, g_final, gla_w_q, gla_w_k, gla_w_v, gla_w_g1, gla_w_g2, gla_b_g, gla_w_r, gla_b_r, gla_g_o, gla_w_o, fox_w_q, fox_w_k, fox_w_v, fox_w_f, fox_b_f, fox_w_o, peer_w_q, peer_keys, peer_u, peer_v):
    batch, seq, d = x_prompt.shape
    db, t_new, _ = x_sample.shape
    n_p, n_s = batch * seq, db * t_new
    xp = x_prompt.reshape(n_p, d)
    xs = x_sample.reshape(n_s, d)
    row = lambda a: a.reshape(1, -1).astype(F32)
    gfin = row(g_final)

    peer = []
    for i in range(peer_w_q.shape[0]):
        peer.append((peer_w_q[i].T.astype(BF16),
                     peer_keys[i].reshape(2 * PEER_HEADS, PEER_NKEYS, PEER_NKEYS).astype(BF16),
                     peer_u[i].astype(BF16), peer_v[i].T.astype(BF16)))

    rank = gla_w_g1.shape[-1]
    wg1 = jnp.pad(gla_w_g1[0], ((0, 0), (0, LANES - rank))).astype(BF16)
    wg2 = jnp.pad(gla_w_g2[0], ((0, LANES - rank), (0, 0))).astype(BF16)
    gla_w = (row(g_mix[0]), gla_w_q[0].astype(BF16), gla_w_k[0].astype(BF16), gla_w_v[0].astype(BF16),
             gla_w_r[0].astype(BF16), wg1, wg2, row(gla_b_g[0]), row(gla_b_r[0]))
    wo0 = gla_w_o[0].astype(BF16)

    def gla_layer(x, b, s, s0t, chunk, sub, pad_to):
        n = x.shape[0]
        q, k, v, lg, r = _gla_proj(x, *gla_w, _tile(n, 512))
        shp = lambda a: a.reshape(b, s, a.shape[-1])
        q, k, v, lg = shp(q), shp(k), shp(v), shp(lg)
        if pad_to > s:
            padseq = lambda a: jnp.pad(a, ((0, 0), (0, pad_to - s), (0, 0)))
            q, k, v, lg = padseq(q), padseq(k), padseq(v), padseq(lg)
        o, sfin_t = _gla_scan(q, k, v, lg, s0t, chunk, sub)
        o = o[:, :s].reshape(n, -1)
        x1, xnt = _post_mix(o, r, row(gla_g_o[0]), wo0, x, row(g_ffn[0]), _tile(n, 512))
        return _peer(x1, xnt, *peer[0], gfin, False), jnp.swapaxes(sfin_t, -1, -2)

    s0p = jnp.zeros((batch, GLA_HEADS, GLA_DV, GLA_DK), F32)
    xp, gla_sp = gla_layer(xp, batch, seq, s0p, GLA_CHUNK, GLA_SUB, seq)
    s0s = jnp.swapaxes(state_gla[0], -1, -2).astype(F32)
    xs, gla_ss = gla_layer(xs, db, t_new, s0s, 8, 8, 8)

    wf = jnp.pad(fox_w_f[0], ((0, 0), (0, LANES - FOX_HEADS))).astype(BF16)
    bf = jnp.pad(fox_b_f[0], (0, LANES - FOX_HEADS)).reshape(1, LANES).astype(F32)
    fox_w = (row(g_mix[1]), fox_w_q[0].astype(BF16), fox_w_k[0].astype(BF16), fox_w_v[0].astype(BF16), wf, bf)
    wo1 = fox_w_o[0].astype(BF16)

    _, kp, vp, lfp, kaug, qaugt, vpt = _fox_proj(xp, *fox_w, _tile(n_p, 512), seq)

    def prompt_kv(a):
        if a.ndim == 3:
            return a.reshape(batch, FOX_HEADS, FOX_HD, seq).transpose(0, 3, 1, 2)[None]
        return a.reshape(1, batch, seq, FOX_HEADS, FOX_HD)

    opt = _fox_flash(qaugt, kaug, vpt, batch, _tile(seq, 1024))
    x1, xnt = _post_mix(opt, None, None, wo1, xp, row(g_ffn[1]), _tile(n_p, 512))
    yp = _peer(x1, xnt, *peer[1], gfin, True)

    qs, ks, vs, lfs, _, _, _ = _fox_proj(xs, *fox_w, _tile(n_s, 512), t_new)
    page = cache_k.shape[2]
    t_pad = 8
    heads_major = lambda a: a.reshape(db, t_new, FOX_HEADS, -1).transpose(0, 2, 1, 3)
    q_h = jnp.pad(heads_major(qs), ((0, 0), (0, 0), (0, t_pad - t_new), (0, 0)))
    lf_h = heads_major(lfs)
    keys_last = lambda a: jnp.pad(heads_major(a).transpose(0, 1, 3, 2), ((0, 0), (0, 0), (0, 0), (0, page - t_new)))
    k_new, v_new = keys_last(ks), keys_last(vs)
    lft_new = jnp.pad(lf_h[..., 0], ((0, 0), (0, 0), (0, page - t_new)))
    ckt = cache_k.transpose(0, 1, 3, 4, 2)
    cvt = cache_v.transpose(0, 1, 3, 4, 2)
    clft = jnp.swapaxes(cache_logf[0], -1, -2)
    os_h = _fox_sample(q_h, lf_h, k_new, v_new, lft_new, ckt, cvt, clft, page_table, 0, 8)
    ost = os_h[:, :, :t_new, :].transpose(1, 3, 0, 2).reshape(d, n_s)
    x1, xnt = _post_mix(ost, None, None, wo1, xs, row(g_ffn[1]), _tile(n_s, 512))
    ys = _peer(x1, xnt, *peer[1], gfin, True)

    return (yp.reshape(batch, seq, d), ys.reshape(db, t_new, d), gla_sp[None], gla_ss[None],
            prompt_kv(kp), prompt_kv(vp),
            lfp.reshape(1, batch, seq, FOX_HEADS),
            ks.reshape(1, db, t_new, FOX_HEADS, FOX_HD), vs.reshape(1, db, t_new, FOX_HEADS, FOX_HD),
            lfs.reshape(1, db, t_new, FOX_HEADS))
```

```python
import functools

import numpy as np
import jax
import jax.numpy as jnp
from jax import lax
from jax.experimental import pallas as pl
from jax.experimental.pallas import tpu as pltpu

F32 = jnp.float32
BF16 = jnp.bfloat16
HIGHEST = lax.Precision.HIGHEST

D_MODEL = 1024
GLA_HEADS = 4
GLA_DK = 128
GLA_DV = 256
GLA_TAU = 16.0
GLA_CHUNK = 64
GLA_SUB = 16
FOX_HEADS = 16
FOX_HD = 64
FOX_PAIRS = FOX_HEADS * FOX_HD // 128
PEER_HEADS = 8
PEER_NKEYS = 128
PEER_TOPK = 16
PEER_N_EXPERTS = PEER_NKEYS * PEER_NKEYS
RMS_EPS = 1e-6
LOG2E = 1.4426950408889634
LANES = 128
VMEM_LIMIT = 56 * 1024 * 1024


def _params(n_axes, vmem=VMEM_LIMIT):
    return pltpu.CompilerParams(dimension_semantics=("arbitrary",) * n_axes, vmem_limit_bytes=vmem)


def _rms(x, g):
    return x * lax.rsqrt(jnp.mean(x * x, axis=-1, keepdims=True) + RMS_EPS) * g


def _dot(a, b, precision=None):
    return jnp.dot(a, b, preferred_element_type=F32, precision=precision)


def _dot_nt(a, b):
    return lax.dot_general(a, b, (((1,), (1,)), ((), ())), preferred_element_type=F32)


def _dot_tn(a, b):
    return lax.dot_general(a, b, (((0,), (0,)), ((), ())), preferred_element_type=F32)


def _log_sigmoid(z):
    return jnp.minimum(z, 0.0) - jnp.log1p(jnp.exp(-jnp.abs(z)))


def _const_spec(shape):
    return pl.BlockSpec(shape, lambda *_: (0,) * len(shape))


def _gla_proj_kernel(x_ref, g_ref, wq_ref, wk_ref, wv_ref, wr_ref, wg1_ref, wg2_ref, bg_ref, br_ref,
                     q_ref, k_ref, v_ref, lg_ref, r_ref):
    h = _rms(x_ref[...], g_ref[...]).astype(BF16)
    q_ref[...] = _dot(h, wq_ref[...]) * (GLA_DK ** -0.5)
    k_ref[...] = _dot(h, wk_ref[...])
    v_ref[...] = _dot(h, wv_ref[...])
    u = _dot(h, wr_ref[...]) + br_ref[...]
    r_ref[...] = u / (1.0 + jnp.exp(-u))
    t = _dot(h, wg1_ref[...]).astype(BF16)
    z = _dot(t, wg2_ref[...]) + bg_ref[...]
    lg_ref[...] = _log_sigmoid(z) * (1.0 / GLA_TAU)


def _gla_proj(x, g, wq, wk, wv, wr, wg1, wg2, bg, br, tm):
    n, d = x.shape
    gk = GLA_HEADS * GLA_DK
    gv = GLA_HEADS * GLA_DV
    tok = lambda w: pl.BlockSpec((tm, w), lambda i: (i, 0))
    return pl.pallas_call(
        _gla_proj_kernel,
        grid=(n // tm,),
        in_specs=[tok(d), _const_spec((1, d)), _const_spec((d, gk)), _const_spec((d, gk)), _const_spec((d, gv)),
                  _const_spec((d, gv)), _const_spec((d, LANES)), _const_spec((LANES, gk)), _const_spec((1, gk)),
                  _const_spec((1, gv))],
        out_specs=[tok(gk), tok(gk), tok(gv), tok(gk), tok(gv)],
        out_shape=[jax.ShapeDtypeStruct((n, gk), F32), jax.ShapeDtypeStruct((n, gk), F32),
                   jax.ShapeDtypeStruct((n, gv), F32), jax.ShapeDtypeStruct((n, gk), F32),
                   jax.ShapeDtypeStruct((n, gv), F32)],
        compiler_params=_params(1),
        name="gla_proj",
    )(x, g, wq, wk, wv, wr, wg1, wg2, bg, br)


def _gla_scan_kernel(q_ref, k_ref, v_ref, g_ref, s0_ref, o_ref, sfin_ref, st_ref, *, chunk, sub):
    j = pl.program_id(1)

    @pl.when(j == 0)
    def _():
        st_ref[...] = s0_ref[0]

    rows = lax.broadcasted_iota(jnp.int32, (chunk, chunk), 0)
    cols = lax.broadcasted_iota(jnp.int32, (chunk, chunk), 1)
    tril = (rows >= cols).astype(F32)
    t_idx = lax.broadcasted_iota(jnp.int32, (sub, sub, 1), 0)
    s_idx = lax.broadcasted_iota(jnp.int32, (sub, sub, 1), 1)
    causal = t_idx >= s_idx
    ones = jnp.ones((LANES, LANES), BF16)
    n_sub = chunk // sub

    for h in range(GLA_HEADS):
        ks = slice(h * GLA_DK, (h + 1) * GLA_DK)
        vs = slice(h * GLA_DV, (h + 1) * GLA_DV)
        qh = q_ref[0, :, ks]
        kh = k_ref[0, :, ks]
        vh = v_ref[0, :, vs]
        b = _dot(tril, g_ref[0, :, ks], precision=HIGHEST)
        b_last = b[chunk - 1:chunk, :]
        st = st_ref[h]
        o = _dot_nt((qh * jnp.exp(b)).astype(BF16), st.astype(BF16))
        parts = []
        for i in range(n_sub):
            bt = b[i * sub:(i + 1) * sub]
            qi = qh[i * sub:(i + 1) * sub]
            acc0 = jnp.zeros((sub, LANES), F32)
            acc1 = jnp.zeros((sub, LANES), F32)
            for jj in range(i + 1):
                bs = b[jj * sub:(jj + 1) * sub]
                kj = kh[jj * sub:(jj + 1) * sub]
                vj = vh[jj * sub:(jj + 1) * sub]
                diff = bt[:, None, :] - bs[None, :, :]
                if jj == i:
                    diff = jnp.where(causal, diff, -jnp.inf)
                a = (qi[:, None, :] * kj[None, :, :]) * jnp.exp(diff)
                att = _dot(a.reshape(sub * sub, GLA_DK).astype(BF16), ones).reshape(sub, sub, LANES)
                acc0 = acc0 + jnp.sum(att * vj[None, :, :LANES], axis=1)
                acc1 = acc1 + jnp.sum(att * vj[None, :, LANES:], axis=1)
            parts.append(jnp.concatenate([acc0, acc1], axis=1))
        o_ref[0, :, vs] = o + jnp.concatenate(parts, axis=0)
        kd = (kh * jnp.exp(b_last - b)).astype(BF16)
        st_ref[h] = st * jnp.exp(b_last) + _dot_tn(vh.astype(BF16), kd)

    @pl.when(j == pl.num_programs(1) - 1)
    def _():
        sfin_ref[0] = st_ref[...]


def _gla_scan(q, k, v, lg, s0t, chunk, sub):
    b, s, gk = q.shape
    gv = v.shape[-1]
    seq = lambda w: pl.BlockSpec((1, chunk, w), lambda bi, j: (bi, j, 0))
    st_spec = pl.BlockSpec((1, GLA_HEADS, GLA_DV, GLA_DK), lambda bi, j: (bi, 0, 0, 0))
    return pl.pallas_call(
        functools.partial(_gla_scan_kernel, chunk=chunk, sub=sub),
        grid=(b, s // chunk),
        in_specs=[seq(gk), seq(gk), seq(gv), seq(gk), st_spec],
        out_specs=[seq(gv), st_spec],
        out_shape=[jax.ShapeDtypeStruct((b, s, gv), F32),
                   jax.ShapeDtypeStruct((b, GLA_HEADS, GLA_DV, GLA_DK), F32)],
        scratch_shapes=[pltpu.VMEM((GLA_HEADS, GLA_DV, GLA_DK), F32)],
        compiler_params=_params(2),
        name="gla_scan",
    )(q, k, v, lg, s0t)


def _post_mix_kernel(*refs, gla):
    if gla:
        o_ref, r_ref, go_ref, wo_ref, x_ref, gf_ref, x1_ref, xnt_ref = refs
        parts = []
        for h in range(GLA_HEADS):
            vs = slice(h * GLA_DV, (h + 1) * GLA_DV)
            parts.append(_rms(o_ref[:, vs], go_ref[...]) * r_ref[:, vs])
        a = jnp.concatenate(parts, axis=1)
    else:
        o_ref, wo_ref, x_ref, gf_ref, x1_ref, xnt_ref = refs
        a = o_ref[...].T
    x1 = x_ref[...] + _dot(a.astype(BF16), wo_ref[...])
    x1_ref[...] = x1
    xnt_ref[...] = _rms(x1, gf_ref[...]).T.astype(BF16)


def _post_mix(o, r, go, wo, x, gf, tm):
    n, d = x.shape
    tok = pl.BlockSpec((tm, d), lambda i: (i, 0))
    gla = r is not None
    if gla:
        args = (o, r, go, wo, x, gf)
        in_specs = [tok, tok, _const_spec((1, GLA_DV)), _const_spec((d, d)), tok, _const_spec((1, d))]
    else:
        args = (o, wo, x, gf)
        in_specs = [pl.BlockSpec((d, tm), lambda i: (0, i)), _const_spec((d, d)), tok, _const_spec((1, d))]
    return pl.pallas_call(
        functools.partial(_post_mix_kernel, gla=gla),
        grid=(n // tm,),
        in_specs=in_specs,
        out_specs=[tok, pl.BlockSpec((d, tm), lambda i: (0, i))],
        out_shape=[jax.ShapeDtypeStruct((n, d), F32), jax.ShapeDtypeStruct((d, n), BF16)],
        compiler_params=_params(1),
        name="post_mix_gla" if gla else "post_mix_fox",
    )(*args)


_PEER_CAND_ROWS = 80


def _peer_candidates(v1, v2):
    r8 = lax.broadcasted_iota(jnp.int32, (8, 1), 0)
    pieces = []
    for a in range(8):
        z = v1[a:a + 1] + v2[0:8]
        pieces.append(z if a == 0 else jnp.where(r8 < PEER_TOPK // (a + 1), z, -jnp.inf))
    pieces.append(v1[0:1] + v2[8:16])
    pieces.append(v1[8:16] + v2[0:1])
    return jnp.concatenate(pieces, axis=0)


def _peer_route_kernel(xnt_ref, wpqt_ref, keys_ref, s1_ref, s2_ref, tau_ref, top_ref):
    xnt = xnt_ref[...]

    def head(p, carry):
        s = []
        for half in range(2):
            row = pl.multiple_of((2 * p + half) * PEER_NKEYS, PEER_NKEYS)
            qt = _dot(wpqt_ref[pl.ds(row, PEER_NKEYS), :], xnt).astype(BF16)
            sc = _dot(keys_ref[2 * p + half], qt)
            s.append(sc)
            cur = sc
            for rnk in range(PEER_TOPK):
                mx = jnp.max(cur, axis=0, keepdims=True)
                top_ref[half, rnk:rnk + 1, :] = mx
                cur = jnp.where(cur == mx, -jnp.inf, cur)
        v1 = top_ref[0]
        v2 = top_ref[1]
        m = v1[0:1] + v2[0:1]
        v1m = v1 - m
        z = _peer_candidates(v1m, v2)
        cur = z
        for rnk in range(PEER_TOPK):
            tau = jnp.max(cur, axis=0, keepdims=True)
            if rnk + 1 < PEER_TOPK:
                cur = jnp.where(cur == tau, -jnp.inf, cur)
        sel = z >= tau
        log_z = jnp.log(jnp.sum(jnp.where(sel, jnp.exp(z), 0.0), axis=0, keepdims=True))
        z2 = _peer_candidates((v1m - log_z) * LOG2E, v2 * LOG2E)
        tau2 = jnp.min(jnp.where(sel, z2, jnp.inf), axis=0, keepdims=True)
        row = pl.multiple_of(p * PEER_NKEYS, PEER_NKEYS)
        s1_ref[pl.ds(row, PEER_NKEYS), :] = ((s[0] - m) - log_z) * LOG2E
        s2_ref[pl.ds(row, PEER_NKEYS), :] = s[1] * LOG2E
        tau_ref[pl.ds(p, 1), :] = tau2
        return carry

    lax.fori_loop(0, PEER_HEADS, head, 0)


def _peer_route(xnt, wpqt, keys, tn):
    d, n = xnt.shape
    rows = PEER_HEADS * PEER_NKEYS
    return pl.pallas_call(
        _peer_route_kernel,
        grid=(n // tn,),
        in_specs=[pl.BlockSpec((d, tn), lambda i: (0, i)), _const_spec((2 * rows, d)),
                  _const_spec((2 * PEER_HEADS, PEER_NKEYS, PEER_NKEYS))],
        out_specs=[pl.BlockSpec((rows, tn), lambda i: (0, i)), pl.BlockSpec((rows, tn), lambda i: (0, i)),
                   pl.BlockSpec((PEER_HEADS, tn), lambda i: (0, i))],
        out_shape=[jax.ShapeDtypeStruct((rows, n), F32), jax.ShapeDtypeStruct((rows, n), F32),
                   jax.ShapeDtypeStruct((PEER_HEADS, n), F32)],
        scratch_shapes=[pltpu.VMEM((2, PEER_TOPK, tn), F32)],
        compiler_params=_params(1),
        name="peer_route",
    )(xnt, wpqt, keys)


def _peer_dense_kernel(xnt_ref, u_ref, unext_ref, vt_ref, vtprev_ref, s1_ref, s2_ref, tau_ref, x1_ref, gfin_ref,
                       out_ref, acc_ref, act_ref, act0_ref, p_ref, plast_ref, *, te, tn, lc, final_norm):
    e = pl.program_id(1)
    n_e = pl.num_programs(1)
    slot = e % 2
    other = 1 - slot
    n_i1 = te // PEER_NKEYS
    n_chunks = tn // lc
    last = n_chunks - 1
    chunk = lambda k: slice(k * lc, (k + 1) * lc)

    @pl.when(e == 0)
    def _():
        acc_ref[...] = jnp.zeros_like(acc_ref)
        act0_ref[slot] = _dot(u_ref[...], xnt_ref[:, chunk(0)])
        plast_ref[other] = jnp.zeros((te, lc), BF16)

    for k in range(n_chunks):
        cols = chunk(k)
        for il in range(n_i1):
            rows = slice(il * PEER_NKEYS, (il + 1) * PEER_NKEYS)
            i1 = e * n_i1 + il
            w = None
            for p in range(PEER_HEADS):
                z = s2_ref[p * PEER_NKEYS:(p + 1) * PEER_NKEYS, cols] + s1_ref[pl.ds(p * PEER_NKEYS + i1, 1), cols]
                term = jnp.where(z >= tau_ref[p:p + 1, cols], jnp.exp2(z), 0.0)
                w = term if w is None else w + term
            a = act0_ref[slot, rows, :] if k == 0 else act_ref[rows, cols]
            gelu = 0.5 * a * (1.0 + lax.erf(a * (2.0 ** -0.5)))
            pv = (w * gelu).astype(BF16)
            if k == last:
                plast_ref[slot, rows, :] = pv
            else:
                p_ref[rows, cols] = pv
            if il == 0:
                if k < last:
                    act_ref[:, chunk(k + 1)] = _dot(u_ref[...], xnt_ref[:, chunk(k + 1)])
                else:
                    act0_ref[other] = _dot(unext_ref[...], xnt_ref[:, chunk(0)])
                if k > 0:
                    acc_ref[:, chunk(k - 1)] += _dot(vt_ref[...], p_ref[:, chunk(k - 1)])
                else:
                    acc_ref[:, chunk(last)] += _dot(vtprev_ref[...], plast_ref[other])

    @pl.when(e == n_e - 1)
    def _():
        acc_ref[:, chunk(last)] += _dot(vt_ref[...], plast_ref[slot])
        y = x1_ref[...] + acc_ref[...].T
        if final_norm:
            y = _rms(y, gfin_ref[...])
        out_ref[...] = y


def _peer_dense(xnt, u, vt, s1, s2, tau, x1, gfin, tn, te, final_norm):
    d, n = xnt.shape
    rows = PEER_HEADS * PEER_NKEYS
    lc = min(tn, 2 * LANES)
    n_e = PEER_N_EXPERTS // te
    tokT = lambda r: pl.BlockSpec((r, tn), lambda i, e: (0, i))
    tok = pl.BlockSpec((tn, d), lambda i, e: (i, 0))
    return pl.pallas_call(
        functools.partial(_peer_dense_kernel, te=te, tn=tn, lc=lc, final_norm=final_norm),
        grid=(n // tn, n_e),
        in_specs=[tokT(d),
                  pl.BlockSpec((te, d), lambda i, e: (e, 0)),
                  pl.BlockSpec((te, d), lambda i, e: (jnp.minimum(e + 1, n_e - 1), 0)),
                  pl.BlockSpec((d, te), lambda i, e: (0, e)),
                  pl.BlockSpec((d, te), lambda i, e: (0, jnp.maximum(e - 1, 0))),
                  tokT(rows), tokT(rows), tokT(PEER_HEADS), tok, pl.BlockSpec((1, d), lambda i, e: (0, 0))],
        out_specs=tok,
        out_shape=jax.ShapeDtypeStruct((n, d), F32),
        scratch_shapes=[pltpu.VMEM((d, tn), F32), pltpu.VMEM((te, tn), F32), pltpu.VMEM((2, te, lc), F32),
                        pltpu.VMEM((te, tn), BF16), pltpu.VMEM((2, te, lc), BF16)],
        compiler_params=_params(2),
        name="peer_dense",
    )(xnt, u, u, vt, vt, s1, s2, tau, x1, gfin)


_FOX_EXTRA = 6


def _fox_extra_base(head):
    return FOX_HD if head % 2 == 0 else 0


def _fox_selectors():
    width = FOX_HEADS * LANES
    sel_k = np.zeros((3, LANES, width), np.float32)
    sel_q = np.zeros((3, LANES, width), np.float32)
    const_k = np.zeros((1, width), np.float32)
    const_q = np.zeros((1, width), np.float32)
    for hd in range(FOX_HEADS):
        base = hd * LANES + _fox_extra_base(hd)
        for part in range(3):
            sel_k[part, hd, base + part] = 1.0
            const_q[0, base + part] = -1.0
            const_k[0, base + 3 + part] = 1.0
            sel_q[part, hd, base + 3 + part] = 1.0
    return jnp.asarray(sel_k, BF16), jnp.asarray(sel_q, BF16), jnp.asarray(const_k), jnp.asarray(const_q)


def _fox_proj_kernel(x_ref, g_ref, wq_ref, wk_ref, wv_ref, wf_ref, bf_ref, selk_ref, selq_ref, ck_ref, cq_ref,
                     q_ref, k_ref, v_ref, lf_ref, kaug_ref, qaugt_ref, vt_ref, carry_ref, *, tiles_per_seq,
                     kv_feature_major):
    tm = x_ref.shape[0]

    @pl.when(pl.program_id(0) % tiles_per_seq == 0)
    def _():
        carry_ref[...] = jnp.zeros_like(carry_ref)

    h = _rms(x_ref[...], g_ref[...]).astype(BF16)
    q = _dot(h, wq_ref[...]) * (FOX_HD ** -0.5)
    q_ref[...] = q
    k = _dot(h, wk_ref[...])
    v = _dot(h, wv_ref[...])
    v_t = v.T
    if kv_feature_major:
        k_ref[0] = k.T
        v_ref[0] = v_t
    else:
        k_ref[...] = k
        v_ref[...] = v
    vt_ref[...] = v_t.astype(BF16)
    lane = lax.broadcasted_iota(jnp.int32, (tm, LANES), 1)
    lf = jnp.where(lane < FOX_HEADS, _log_sigmoid(_dot(h, wf_ref[...]) + bf_ref[...]), 0.0)
    lf_ref[...] = lf[:, :FOX_HEADS]
    rows = lax.broadcasted_iota(jnp.int32, (tm, tm), 0)
    cols = lax.broadcasted_iota(jnp.int32, (tm, tm), 1)
    c = _dot((rows >= cols).astype(F32), lf, precision=HIGHEST) + carry_ref[...]
    carry_ref[...] = c[tm - 1:tm, :]
    c2 = c * LOG2E
    hi = c2.astype(BF16)
    r1 = c2 - hi.astype(F32)
    mid = r1.astype(BF16)
    lo = (r1 - mid.astype(F32)).astype(BF16)
    ext_k = _dot(hi, selk_ref[0]) + _dot(mid, selk_ref[1]) + _dot(lo, selk_ref[2]) + ck_ref[...]
    ext_q = _dot(hi, selq_ref[0]) + _dot(mid, selq_ref[1]) + _dot(lo, selq_ref[2]) + cq_ref[...]
    for hd in range(FOX_HEADS):
        src = slice((hd // 2) * LANES, (hd // 2 + 1) * LANES)
        dst = slice(hd * LANES, (hd + 1) * LANES)
        own = (lane < FOX_HD) if hd % 2 == 0 else (lane >= FOX_HD)
        kaug_ref[:, dst] = jnp.where(own, k[:, src], ext_k[:, dst]).astype(BF16)
        qaugt_ref[dst, :] = jnp.where(own, q[:, src] * LOG2E, ext_q[:, dst]).T.astype(BF16)


def _fox_proj(x, g, wq, wk, wv, wf, bf, tm, seq):
    n, d = x.shape
    width = FOX_HEADS * LANES
    tok = pl.BlockSpec((tm, d), lambda i: (i, 0))
    sel_k, sel_q, const_k, const_q = _fox_selectors()
    tiles_per_seq = max(seq // tm, 1)
    kv_feature_major = seq % tm == 0
    if kv_feature_major:
        kv_spec = pl.BlockSpec((1, d, tm), lambda i: (i // tiles_per_seq, 0, i % tiles_per_seq))
        kv_shape = jax.ShapeDtypeStruct((n // seq, d, seq), F32)
    else:
        kv_spec, kv_shape = tok, jax.ShapeDtypeStruct((n, d), F32)
    return pl.pallas_call(
        functools.partial(_fox_proj_kernel, tiles_per_seq=tiles_per_seq, kv_feature_major=kv_feature_major),
        grid=(n // tm,),
        in_specs=[tok, _const_spec((1, d)), _const_spec((d, d)), _const_spec((d, d)), _const_spec((d, d)),
                  _const_spec((d, LANES)), _const_spec((1, LANES)), _const_spec((3, LANES, width)),
                  _const_spec((3, LANES, width)), _const_spec((1, width)), _const_spec((1, width))],
        out_specs=[tok, kv_spec, kv_spec, pl.BlockSpec((tm, FOX_HEADS), lambda i: (i, 0)),
                   pl.BlockSpec((tm, width), lambda i: (i, 0)), pl.BlockSpec((width, tm), lambda i: (0, i)),
                   pl.BlockSpec((d, tm), lambda i: (0, i))],
        out_shape=[jax.ShapeDtypeStruct((n, d), F32), kv_shape, kv_shape, jax.ShapeDtypeStruct((n, FOX_HEADS), F32),
                   jax.ShapeDtypeStruct((n, width), BF16), jax.ShapeDtypeStruct((width, n), BF16),
                   jax.ShapeDtypeStruct((d, n), BF16)],
        scratch_shapes=[pltpu.VMEM((1, LANES), F32)],
        compiler_params=_params(1),
        name="fox_proj",
    )(x, g, wq, wk, wv, wf, bf, sel_k, sel_q, const_k, const_q)


def _fox_flash_kernel(qi_ref, ki_ref, qt_ref, k_ref, vt_ref, o_ref, s_ref, m_ref, l_ref, acc_ref, *, tq, tk):
    t = pl.program_id(2)
    qi = qi_ref[t]
    ki = ki_ref[t]

    @pl.when(ki == 0)
    def _():
        m_ref[...] = jnp.full_like(m_ref, -jnp.inf)
        l_ref[...] = jnp.zeros_like(l_ref)
        acc_ref[...] = jnp.zeros_like(acc_ref)

    half = tq // 2
    parts = [(hh, c) for hh in range(2) for c in range(2)]

    def step(masked):
        n_keys = lambda c: half if (masked and c == 0 and tk == tq) else tk

        def logits(i):
            hh, c = parts[i]
            nk = n_keys(c)
            feat = slice(hh * LANES, (hh + 1) * LANES)
            s = _dot(k_ref[:nk, feat], qt_ref[feat, c * half:(c + 1) * half])
            if masked:
                key = lax.broadcasted_iota(jnp.int32, (nk, half), 0)
                query = lax.broadcasted_iota(jnp.int32, (nk, half), 1) + c * half
                s = jnp.where(key <= query, s, -jnp.inf)
            s_ref[hh, :nk, c * half:(c + 1) * half] = s
            return jnp.max(s, axis=0, keepdims=True)

        def update(i, m_cur):
            hh, c = parts[i]
            nk = n_keys(c)
            cols = slice(c * half, (c + 1) * half)
            m_prev = m_ref[hh, :, cols]
            m_new = jnp.maximum(m_prev, m_cur)
            alpha = jnp.exp2(m_prev - m_new)
            p = jnp.exp2(s_ref[hh, :nk, cols] - m_new)
            l_ref[hh, :, cols] = alpha * l_ref[hh, :, cols] + jnp.sum(p, axis=0, keepdims=True)
            m_ref[hh, :, cols] = m_new
            acc_ref[hh, :, cols] = (acc_ref[hh, :, cols] * alpha
                                    + _dot(vt_ref[hh * FOX_HD:(hh + 1) * FOX_HD, :nk], p.astype(BF16)))

        m_cur = [logits(0), logits(1)]
        for i in range(len(parts)):
            update(i, m_cur[i])
            if i + 2 < len(parts):
                m_cur.append(logits(i + 2))

    @pl.when(ki < qi)
    def _():
        step(False)

    @pl.when(ki == qi)
    def _():
        step(True)
        for hh in range(2):
            o_ref[hh * FOX_HD:(hh + 1) * FOX_HD, :] = acc_ref[hh] / l_ref[hh]


def _fox_flash(qaugt, kaug, vt, batch, tq):
    d, n = vt.shape
    s = n // batch
    nq = s // tq
    tri = [(qi, ki) for qi in range(nq) for ki in range(qi + 1)]
    qi_of = jnp.array([a for a, _ in tri], jnp.int32)
    ki_of = jnp.array([b for _, b in tri], jnp.int32)
    grid_spec = pltpu.PrefetchScalarGridSpec(
        num_scalar_prefetch=2,
        grid=(batch, FOX_PAIRS, len(tri)),
        in_specs=[pl.BlockSpec((2 * LANES, tq), lambda b, pr, t, qi, ki: (pr, b * nq + qi[t])),
                  pl.BlockSpec((tq, 2 * LANES), lambda b, pr, t, qi, ki: (b * nq + ki[t], pr)),
                  pl.BlockSpec((LANES, tq), lambda b, pr, t, qi, ki: (pr, b * nq + ki[t]))],
        out_specs=pl.BlockSpec((LANES, tq), lambda b, pr, t, qi, ki: (pr, b * nq + qi[t])),
        scratch_shapes=[pltpu.VMEM((2, tq, tq), F32), pltpu.VMEM((2, 1, tq), F32), pltpu.VMEM((2, 1, tq), F32),
                        pltpu.VMEM((2, FOX_HD, tq), F32)],
    )
    return pl.pallas_call(
        functools.partial(_fox_flash_kernel, tq=tq, tk=tq),
        grid_spec=grid_spec,
        out_shape=jax.ShapeDtypeStruct((d, n), F32),
        compiler_params=_params(3),
        name="fox_flash",
    )(qi_of, ki_of, qaugt, kaug, vt)


def _fox_sample_kernel(pt_ref, q_ref, lfc_ref, kn_ref, vn_ref, lfn_ref, *rest, pages_per_step, t_new):
    k_refs = rest[:pages_per_step]
    v_refs = rest[pages_per_step:2 * pages_per_step]
    lf_refs = rest[2 * pages_per_step:3 * pages_per_step]
    o_ref, cnb_ref, m_ref, l_ref, acc_ref, tot_ref = rest[3 * pages_per_step:]
    step = pl.program_id(1)
    t_pad = q_ref.shape[2]
    page = lf_refs[0].shape[-1]
    rows_i = lax.broadcasted_iota(jnp.int32, (page, page), 0)
    cols_i = lax.broadcasted_iota(jnp.int32, (page, page), 1)

    q = q_ref[0].astype(BF16)

    def per_head_rows(a):
        return jnp.stack([jnp.broadcast_to(a[h:h + 1, :], (t_pad, page)) for h in range(FOX_HEADS)], axis=0)

    def attend(blocks):
        s = [lax.dot_general(q, kt.astype(BF16), (((2,), (1,)), ((0,), (0,))), preferred_element_type=F32) + bias
             for kt, _, bias in blocks]
        m_prev = m_ref[...]
        m_new = m_prev
        for si in s:
            m_new = jnp.maximum(m_new, jnp.max(si, axis=2, keepdims=True))
        alpha = jnp.exp(m_prev - m_new)
        l_new = alpha * l_ref[...]
        acc = acc_ref[...] * alpha[:, :, :FOX_HD]
        for si, (_, vt, _) in zip(s, blocks):
            p = jnp.exp(si - m_new)
            l_new = l_new + jnp.sum(p, axis=2, keepdims=True)
            acc = acc + lax.dot_general(p.astype(BF16), vt.astype(BF16), (((2,), (2,)), ((0,), (0,))),
                                        preferred_element_type=F32)
        l_ref[...] = l_new
        m_ref[...] = m_new
        acc_ref[...] = acc

    @pl.when(step == 0)
    def _():
        m_ref[...] = jnp.full_like(m_ref, -jnp.inf)
        l_ref[...] = jnp.zeros_like(l_ref)
        acc_ref[...] = jnp.zeros_like(acc_ref)
        tot_ref[...] = jnp.zeros_like(tot_ref)
        col = lfc_ref[0]
        row_t = lax.broadcasted_iota(jnp.int32, (1, t_pad, 1), 1)
        run = jnp.zeros((FOX_HEADS, 1, 1), F32)
        cn = jnp.zeros((FOX_HEADS, t_pad, 1), F32)
        for t in range(t_new):
            run = run + col[:, t:t + 1, :]
            cn = jnp.where(row_t == t, run, cn)
        cnb_ref[...] = jnp.broadcast_to(cn, cnb_ref.shape)
        cnt = _dot(lfn_ref[0], (rows_i <= cols_i).astype(F32), precision=HIGHEST)
        causal = (lax.broadcasted_iota(jnp.int32, (1, t_pad, page), 2)
                  <= lax.broadcasted_iota(jnp.int32, (1, t_pad, page), 1))
        attend([(kn_ref[0], vn_ref[0], jnp.where(causal, cnb_ref[...] - per_head_rows(cnt), -jnp.inf))])

    strict = (rows_i > cols_i).astype(F32)
    ones = jnp.ones((page, page), F32)
    tot = tot_ref[...]
    blocks = []
    for i in range(pages_per_step):
        lft = lf_refs[i][0]
        r = _dot(lft, strict, precision=HIGHEST) + tot
        tot = tot + _dot(lft, ones, precision=HIGHEST)
        blocks.append((k_refs[i][0, 0], v_refs[i][0, 0], cnb_ref[...] + per_head_rows(r)))
    tot_ref[...] = tot
    attend(blocks)

    @pl.when(step == pl.num_programs(1) - 1)
    def _():
        o_ref[0] = acc_ref[...] / l_ref[...][:, :, :FOX_HD]


def _fox_sample(q, lf_col, k_new, v_new, lft_new, cache_k, cache_v, cache_lft, page_table, layer, pages_per_step):
    db, _, t_pad, _ = q.shape
    t_new = lf_col.shape[2]
    n_pages = page_table.shape[1]
    page = cache_k.shape[-1]
    assert page == LANES and cache_k.shape[2:4] == (FOX_HEADS, FOX_HD)
    n_steps = n_pages // pages_per_step
    lf_col = jnp.pad(lf_col, ((0, 0), (0, 0), (0, t_pad - t_new), (0, 0)))

    def page_of(i):
        return lambda b, s, pt: pt[b, n_pages - 1 - (s * pages_per_step + i)]

    def kv_map(i):
        idx = page_of(i)
        return lambda b, s, pt: (layer, idx(b, s, pt), 0, 0, 0)

    def lf_map(i):
        idx = page_of(i)
        return lambda b, s, pt: (idx(b, s, pt), 0, 0)

    per_b = lambda shape: pl.BlockSpec((1,) + shape, lambda b, s, pt: (b,) + (0,) * len(shape))
    kv_blk = (FOX_HEADS, FOX_HD, page)
    in_specs = [per_b((FOX_HEADS, t_pad, FOX_HD)), per_b((FOX_HEADS, t_pad, 1)), per_b(kv_blk), per_b(kv_blk),
                per_b((FOX_HEADS, page))]
    in_specs += [pl.BlockSpec((1, 1) + kv_blk, kv_map(i)) for i in range(pages_per_step)]
    in_specs += [pl.BlockSpec((1, 1) + kv_blk, kv_map(i)) for i in range(pages_per_step)]
    in_specs += [pl.BlockSpec((1, FOX_HEADS, page), lf_map(i)) for i in range(pages_per_step)]
    grid_spec = pltpu.PrefetchScalarGridSpec(
        num_scalar_prefetch=1,
        grid=(db, n_steps),
        in_specs=in_specs,
        out_specs=per_b((FOX_HEADS, t_pad, FOX_HD)),
        scratch_shapes=[pltpu.VMEM((FOX_HEADS, t_pad, LANES), F32), pltpu.VMEM((FOX_HEADS, t_pad, LANES), F32),
                        pltpu.VMEM((FOX_HEADS, t_pad, LANES), F32), pltpu.VMEM((FOX_HEADS, t_pad, FOX_HD), F32),
                        pltpu.VMEM((FOX_HEADS, page), F32)],
    )
    return pl.pallas_call(
        functools.partial(_fox_sample_kernel, pages_per_step=pages_per_step, t_new=t_new),
        grid_spec=grid_spec,
        out_shape=jax.ShapeDtypeStruct((db, FOX_HEADS, t_pad, FOX_HD), F32),
        compiler_params=_params(2),
        name="fox_sample",
    )(page_table, q, lf_col, k_new, v_new, lft_new,
      *([cache_k] * pages_per_step), *([cache_v] * pages_per_step), *([cache_lft] * pages_per_step))


def _tile(n, pref):
    return pref if n % pref == 0 else n


def _peer(x1, xnt, wpqt, keys, u, vt, gfin, final_norm):
    n = x1.shape[0]
    s1, s2, tau = _peer_route(xnt, wpqt, keys, _tile(n, 1024))
    return _peer_dense(xnt, u, vt, s1, s2, tau, x1, gfin, _tile(n, 1024), 512, final_norm)


def kernel(x_prompt, x_sample, state_gla, cache_k, cache_v, cache_logf, page_table, g_mix, g_ffn, g_final, gla_w_q, gla_w_k, gla_w_v, gla_w_g1, gla_w_g2, gla_b_g, gla_w_r, gla_b_r, gla_g_o, gla_w_o, fox_w_q, fox_w_k, fox_w_v, fox_w_f, fox_b_f, fox_w_o, peer_w_q, peer_keys, peer_u, peer_v):
    batch, seq, d = x_prompt.shape
    db, t_new, _ = x_sample.shape
    n_p, n_s = batch * seq, db * t_new
    xp = x_prompt.reshape(n_p, d)
    xs = x_sample.reshape(n_s, d)
    row = lambda a: a.reshape(1, -1).astype(F32)
    gfin = row(g_final)

    peer = []
    for i in range(peer_w_q.shape[0]):
        peer.append((peer_w_q[i].T.astype(BF16),
                     peer_keys[i].reshape(2 * PEER_HEADS, PEER_NKEYS, PEER_NKEYS).astype(BF16),
                     peer_u[i].astype(BF16), peer_v[i].T.astype(BF16)))

    rank = gla_w_g1.shape[-1]
    wg1 = jnp.pad(gla_w_g1[0], ((0, 0), (0, LANES - rank))).astype(BF16)
    wg2 = jnp.pad(gla_w_g2[0], ((0, LANES - rank), (0, 0))).astype(BF16)
    gla_w = (row(g_mix[0]), gla_w_q[0].astype(BF16), gla_w_k[0].astype(BF16), gla_w_v[0].astype(BF16),
             gla_w_r[0].astype(BF16), wg1, wg2, row(gla_b_g[0]), row(gla_b_r[0]))
    wo0 = gla_w_o[0].astype(BF16)

    def gla_layer(x, b, s, s0t, chunk, sub, pad_to):
        n = x.shape[0]
        q, k, v, lg, r = _gla_proj(x, *gla_w, _tile(n, 512))
        shp = lambda a: a.reshape(b, s, a.shape[-1])
        q, k, v, lg = shp(q), shp(k), shp(v), shp(lg)
        if pad_to > s:
            padseq = lambda a: jnp.pad(a, ((0, 0), (0, pad_to - s), (0, 0)))
            q, k, v, lg = padseq(q), padseq(k), padseq(v), padseq(lg)
        o, sfin_t = _gla_scan(q, k, v, lg, s0t, chunk, sub)
        o = o[:, :s].reshape(n, -1)
        x1, xnt = _post_mix(o, r, row(gla_g_o[0]), wo0, x, row(g_ffn[0]), _tile(n, 512))
        return _peer(x1, xnt, *peer[0], gfin, False), jnp.swapaxes(sfin_t, -1, -2)

    s0p = jnp.zeros((batch, GLA_HEADS, GLA_DV, GLA_DK), F32)
    xp, gla_sp = gla_layer(xp, batch, seq, s0p, GLA_CHUNK, GLA_SUB, seq)
    s0s = jnp.swapaxes(state_gla[0], -1, -2).astype(F32)
    xs, gla_ss = gla_layer(xs, db, t_new, s0s, 8, 8, 8)

    wf = jnp.pad(fox_w_f[0], ((0, 0), (0, LANES - FOX_HEADS))).astype(BF16)
    bf = jnp.pad(fox_b_f[0], (0, LANES - FOX_HEADS)).reshape(1, LANES).astype(F32)
    fox_w = (row(g_mix[1]), fox_w_q[0].astype(BF16), fox_w_k[0].astype(BF16), fox_w_v[0].astype(BF16), wf, bf)
    wo1 = fox_w_o[0].astype(BF16)

    _, kp, vp, lfp, kaug, qaugt, vpt = _fox_proj(xp, *fox_w, _tile(n_p, 512), seq)

    def prompt_kv(a):
        if a.ndim == 3:
            return a.reshape(batch, FOX_HEADS, FOX_HD, seq).transpose(0, 3, 1, 2)[None]
        return a.reshape(1, batch, seq, FOX_HEADS, FOX_HD)

    opt = _fox_flash(qaugt, kaug, vpt, batch, _tile(seq, 1024))
    x1, xnt = _post_mix(opt, None, None, wo1, xp, row(g_ffn[1]), _tile(n_p, 512))
    yp = _peer(x1, xnt, *peer[1], gfin, True)

    qs, ks, vs, lfs, _, _, _ = _fox_proj(xs, *fox_w, _tile(n_s, 512), t_new)
    page = cache_k.shape[2]
    t_pad = 8
    heads_major = lambda a: a.reshape(db, t_new, FOX_HEADS, -1).transpose(0, 2, 1, 3)
    q_h = jnp.pad(heads_major(qs), ((0, 0), (0, 0), (0, t_pad - t_new), (0, 0)))
    lf_h = heads_major(lfs)
    keys_last = lambda a: jnp.pad(heads_major(a).transpose(0, 1, 3, 2), ((0, 0), (0, 0), (0, 0), (0, page - t_new)))
    k_new, v_new = keys_last(ks), keys_last(vs)
    lft_new = jnp.pad(lf_h[..., 0], ((0, 0), (0, 0), (0, page - t_new)))
    ckt = cache_k.transpose(0, 1, 3, 4, 2)
    cvt = cache_v.transpose(0, 1, 3, 4, 2)
    clft = jnp.swapaxes(cache_logf[0], -1, -2)
    os_h = _fox_sample(q_h, lf_h, k_new, v_new, lft_new, ckt, cvt, clft, page_table, 0, 8)
    ost = os_h[:, :, :t_new, :].transpose(1, 3, 0, 2).reshape(d, n_s)
    x1, xnt = _post_mix(ost, None, None, wo1, xs, row(g_ffn[1]), _tile(n_s, 512))
    ys = _peer(x1, xnt, *peer[1], gfin, True)

    return (yp.reshape(batch, seq, d), ys.reshape(db, t_new, d), gla_sp[None], gla_ss[None],
            prompt_kv(kp), prompt_kv(vp),
            lfp.reshape(1, batch, seq, FOX_HEADS),
            ks.reshape(1, db, t_new, FOX_HEADS, FOX_HD), vs.reshape(1, db, t_new, FOX_HEADS, FOX_HD),
            lfs.reshape(1, db, t_new, FOX_HEADS))
```
